```python
import math
import jax, jax.numpy as jnp
from jax import lax
import numpy as np

D_MODEL = 4096
BATCH = 1
SEQ = 16384
DEPTH = 2

N_MIXERS = 2
N_EVEN = (DEPTH + 1) // 2
N_ODD = DEPTH // 2

HEAD_DIM = 128
N_Q_HEADS = D_MODEL // HEAD_DIM
N_KV_HEADS = N_Q_HEADS // 4
Q_PER_KV = N_Q_HEADS // N_KV_HEADS
WINDOW = 128
BLOCK = 128
QKV_DIM = (N_Q_HEADS + 2 * N_KV_HEADS) * HEAD_DIM
ATTN_SCALE = HEAD_DIM ** -0.5

SSM_GROUP = 16
SSM_GROUPS = D_MODEL // SSM_GROUP
SSM_STATE = 64
SCAN_CHUNK = 128
DT_MIN = 1e-3
DT_MAX = 1e-1

D_FF = D_MODEL * 7 // 2
N_EXPERTS = 8
TOP_K = 2
D_FF_EXPERT = D_MODEL * 3 // 2

RMS_EPS = 1e-5
NEG_INF = -1e30

kernel_name = "hybrid_swa_sink_s5_moe_trunk"


def rms_norm(x, g):
    xf = x.astype(jnp.float32)
    y = xf * lax.rsqrt(jnp.mean(xf * xf, axis=-1, keepdims=True) + RMS_EPS)
    return (y * g.astype(jnp.float32)).astype(x.dtype)


def swiglu(h, w_gate, w_up, w_down):
    return (jax.nn.silu(h @ w_gate) * (h @ w_up)) @ w_down


def sliding_window_attention(h, w_qkv, q_gain, k_gain, sinks, w_o):
    b, s, _ = h.shape
    nb = s // BLOCK
    qkv = h @ w_qkv
    q_end = N_Q_HEADS * HEAD_DIM
    k_end = q_end + N_KV_HEADS * HEAD_DIM
    q = qkv[..., :q_end].reshape(b, s, N_KV_HEADS, Q_PER_KV, HEAD_DIM)
    k = qkv[..., q_end:k_end].reshape(b, s, N_KV_HEADS, HEAD_DIM)
    v = qkv[..., k_end:].reshape(b, s, N_KV_HEADS, HEAD_DIM)
    q = rms_norm(q, q_gain)
    k = rms_norm(k, k_gain)

    qb = q.reshape(b, nb, BLOCK, N_KV_HEADS, Q_PER_KV, HEAD_DIM)
    kb = k.reshape(b, nb, BLOCK, N_KV_HEADS, HEAD_DIM)
    vb = v.reshape(b, nb, BLOCK, N_KV_HEADS, HEAD_DIM)
    pad = ((0, 0), (1, 0), (0, 0), (0, 0), (0, 0))
    kk = jnp.concatenate([jnp.pad(kb, pad)[:, :-1], kb], axis=2)
    vv = jnp.concatenate([jnp.pad(vb, pad)[:, :-1], vb], axis=2)

    scores = jnp.einsum('bnqhgd,bnkhd->bnhgqk', qb, kk,
                        preferred_element_type=jnp.float32) * ATTN_SCALE
    qi = jnp.arange(BLOCK)[:, None]
    kj = jnp.arange(2 * BLOCK)[None, :]
    diff = qi + BLOCK - kj
    band = (diff >= 0) & (diff < WINDOW)
    kpos = jnp.arange(nb)[:, None, None] * BLOCK + kj[None] - BLOCK
    valid = band[None] & (kpos >= 0)
    scores = jnp.where(valid[None, :, None, None], scores, NEG_INF)

    sink = sinks.astype(jnp.float32).reshape(N_KV_HEADS, Q_PER_KV)[None, None, :, :, None, None]
    m = jnp.maximum(jnp.max(scores, axis=-1, keepdims=True), sink)
    e = jnp.exp(scores - m)
    denom = jnp.sum(e, axis=-1, keepdims=True) + jnp.exp(sink - m)
    p = e / denom
    o = jnp.einsum('bnhgqk,bnkhd->bnqhgd', p.astype(vv.dtype), vv)
    return o.reshape(b, s, N_Q_HEADS * HEAD_DIM) @ w_o


def cmul(ar, ai, br, bi):
    return ar * br - ai * bi, ar * bi + ai * br


def ssm_combine(e1, e2):
    a1r, a1i, b1r, b1i = e1
    a2r, a2i, b2r, b2i = e2
    ar, ai = cmul(a2r, a2i, a1r, a1i)
    br, bi = cmul(a2r, a2i, b1r, b1i)
    return ar, ai, br + b2r, bi + b2i


def s5_layer(h, lam_re, lam_im, log_step, b_re, b_im, c_re, c_im, d_skip, w_glu):
    b, s, d = h.shape
    nc = s // SCAN_CHUNK
    lam_re = lam_re.astype(jnp.float32)
    lam_im = lam_im.astype(jnp.float32)
    step = jnp.exp(log_step.astype(jnp.float32))[:, None]
    mag = jnp.exp(lam_re * step)
    lb_re = mag * jnp.cos(lam_im * step)
    lb_im = mag * jnp.sin(lam_im * step)
    inv = 1.0 / (lam_re * lam_re + lam_im * lam_im)
    coef_re = ((lb_re - 1.0) * lam_re + lb_im * lam_im) * inv
    coef_im = (lb_im * lam_re - (lb_re - 1.0) * lam_im) * inv
    br32, bi32 = b_re.astype(jnp.float32), b_im.astype(jnp.float32)
    bb_re = coef_re[..., None] * br32 - coef_im[..., None] * bi32
    bb_im = coef_re[..., None] * bi32 + coef_im[..., None] * br32
    cr32, ci32 = c_re.astype(jnp.float32), c_im.astype(jnp.float32)

    u = h.astype(jnp.float32)
    u_chunks = u.reshape(b, nc, SCAN_CHUNK, SSM_GROUPS, SSM_GROUP).transpose(1, 2, 0, 3, 4)

    def chunk_step(carry, u_c):
        h_re, h_im = carry
        bu_re = jnp.einsum('lbgc,gpc->lbgp', u_c, bb_re)
        bu_im = jnp.einsum('lbgc,gpc->lbgp', u_c, bb_im)
        cr, ci = cmul(lb_re, lb_im, h_re, h_im)
        bu_re = bu_re.at[0].add(cr)
        bu_im = bu_im.at[0].add(ci)
        a_re = jnp.broadcast_to(lb_re, bu_re.shape)
        a_im = jnp.broadcast_to(lb_im, bu_im.shape)
        _, _, st_re, st_im = lax.associative_scan(ssm_combine, (a_re, a_im, bu_re, bu_im), axis=0)
        y = (jnp.einsum('lbgp,gcp->lbgc', st_re, cr32)
             - jnp.einsum('lbgp,gcp->lbgc', st_im, ci32))
        return (st_re[-1], st_im[-1]), y

    init = (jnp.zeros((b, SSM_GROUPS, SSM_STATE), jnp.float32),
            jnp.zeros((b, SSM_GROUPS, SSM_STATE), jnp.float32))
    _, y = lax.scan(chunk_step, init, u_chunks)
    y = y.transpose(2, 0, 1, 3, 4).reshape(b, s, d)
    y = jax.nn.gelu(y + d_skip.astype(jnp.float32) * u).astype(h.dtype)
    gl = y @ w_glu
    return gl[..., :d] * jax.nn.sigmoid(gl[..., d:])


def moe_swiglu(h, w_router, w_gate_e, w_up_e, w_down_e):
    b, s, d = h.shape
    t = h.reshape(b * s, d)
    logits = jnp.matmul(t, w_router, preferred_element_type=jnp.float32)
    top_logits, top_idx = lax.top_k(logits, TOP_K)
    top_w = jax.nn.softmax(top_logits, axis=-1)
    gate = jnp.sum(jax.nn.one_hot(top_idx, N_EXPERTS, dtype=jnp.float32) * top_w[..., None], axis=1)
    out = jnp.zeros((b * s, d), jnp.float32)
    for e in range(N_EXPERTS):
        y_e = swiglu(t, w_gate_e[e], w_up_e[e], w_down_e[e])
        out = out + gate[:, e:e + 1] * y_e.astype(jnp.float32)
    return out.astype(h.dtype).reshape(b, s, d)


def setup_inputs(seed: int = 0) -> dict:
    key = jax.random.key(seed)
    ks = jax.random.split(key, 26)
    f32 = jnp.float32

    def nrm(k, shape, scale):
        return jax.random.normal(k, shape, f32) * scale

    def gain(k, shape):
        return 1.0 + 0.02 * jax.random.normal(k, shape, f32)

    lam_re = -0.5 + 0.01 * jax.random.normal(ks[11], (N_ODD, SSM_GROUPS, SSM_STATE), f32)
    lam_im = (math.pi * jnp.arange(SSM_STATE, dtype=f32)
              + 0.01 * jax.random.normal(ks[12], (N_ODD, SSM_GROUPS, SSM_STATE), f32))
    log_step = jax.random.uniform(ks[13], (N_ODD, SSM_GROUPS), f32,
                                  minval=math.log(DT_MIN), maxval=math.log(DT_MAX))
    b_scale = (2.0 * SSM_GROUP) ** -0.5
    c_scale = (2.0 * SSM_STATE) ** -0.5
    return {
        "x": jax.random.normal(ks[0], (BATCH, SEQ, D_MODEL), f32),
        "attn_norm": gain(ks[1], (N_EVEN, D_MODEL)),
        "w_qkv": nrm(ks[2], (N_EVEN, D_MODEL, QKV_DIM), D_MODEL ** -0.5),
        "q_norm": gain(ks[3], (N_EVEN, HEAD_DIM)),
        "k_norm": gain(ks[4], (N_EVEN, HEAD_DIM)),
        "sinks": nrm(ks[5], (N_EVEN, N_Q_HEADS), 0.5),
        "w_o": nrm(ks[6], (N_EVEN, N_Q_HEADS * HEAD_DIM, D_MODEL), (N_Q_HEADS * HEAD_DIM) ** -0.5),
        "ffn_norm": gain(ks[7], (N_EVEN, D_MODEL)),
        "w_gate": nrm(ks[8], (N_EVEN, D_MODEL, D_FF), D_MODEL ** -0.5),
        "w_up": nrm(ks[9], (N_EVEN, D_MODEL, D_FF), D_MODEL ** -0.5),
        "w_down": nrm(ks[10], (N_EVEN, D_FF, D_MODEL), D_FF ** -0.5),
        "ssm_norm": gain(ks[14], (N_ODD, D_MODEL)),
        "lam_re": lam_re,
        "lam_im": lam_im,
        "log_step": log_step,
        "b_re": nrm(ks[15], (N_ODD, SSM_GROUPS, SSM_STATE, SSM_GROUP), b_scale),
        "b_im": nrm(ks[16], (N_ODD, SSM_GROUPS, SSM_STATE, SSM_GROUP), b_scale),
        "c_re": nrm(ks[17], (N_ODD, SSM_GROUPS, SSM_GROUP, SSM_STATE), c_scale),
        "c_im": nrm(ks[18], (N_ODD, SSM_GROUPS, SSM_GROUP, SSM_STATE), c_scale),
        "d_skip": nrm(ks[19], (N_ODD, D_MODEL), 1.0),
        "w_glu": nrm(ks[20], (N_ODD, D_MODEL, 2 * D_MODEL), D_MODEL ** -0.5),
        "moe_norm": gain(ks[21], (N_ODD, D_MODEL)),
        "w_router": nrm(ks[22], (N_ODD, D_MODEL, N_EXPERTS), D_MODEL ** -0.5),
        "w_gate_e": nrm(ks[23], (N_ODD, N_EXPERTS, D_MODEL, D_FF_EXPERT), D_MODEL ** -0.5),
        "w_up_e": nrm(ks[24], (N_ODD, N_EXPERTS, D_MODEL, D_FF_EXPERT), D_MODEL ** -0.5),
        "w_down_e": nrm(ks[25], (N_ODD, N_EXPERTS, D_FF_EXPERT, D_MODEL), D_FF_EXPERT ** -0.5),
    }


def reference(x, attn_norm, w_qkv, q_norm, k_norm, sinks, w_o,
              ffn_norm, w_gate, w_up, w_down,
              ssm_norm, lam_re, lam_im, log_step, b_re, b_im, c_re, c_im, d_skip, w_glu,
              moe_norm, w_router, w_gate_e, w_up_e, w_down_e):
    for i in range(DEPTH):
        j = i // N_MIXERS
        if i % N_MIXERS == 0:
            x = x + sliding_window_attention(rms_norm(x, attn_norm[j]), w_qkv[j],
                                             q_norm[j], k_norm[j], sinks[j], w_o[j])
            x = x + swiglu(rms_norm(x, ffn_norm[j]), w_gate[j], w_up[j], w_down[j])
        else:
            x = x + s5_layer(rms_norm(x, ssm_norm[j]), lam_re[j], lam_im[j], log_step[j],
                             b_re[j], b_im[j], c_re[j], c_im[j], d_skip[j], w_glu[j])
            x = x + moe_swiglu(rms_norm(x, moe_norm[j]), w_router[j],
                               w_gate_e[j], w_up_e[j], w_down_e[j])
    return x
```

```python
import functools

import jax
import jax.numpy as jnp
from jax import lax
from jax.experimental import pallas as pl
from jax.experimental.pallas import tpu as pltpu

F32 = jnp.float32
BF16 = jnp.bfloat16
HIGHEST = lax.Precision.HIGHEST

RMS_EPS = 1e-5
NEG_INF = -1e30

HEAD_DIM = 128
Q_PER_KV = 4
ATTN_BLOCK = 128
SSM_GROUP = 16
SSM_STATE = 64
SSM_CHUNK = 16
TOP_K = 2

LANES = 128
VMEM_LIMIT_MB = 56


def _cparams(semantics, vmem_mb=VMEM_LIMIT_MB):
    return pltpu.CompilerParams(dimension_semantics=semantics,
                                vmem_limit_bytes=vmem_mb << 20)


def _tile(n, pref, quantum=128):
    if n <= pref:
        return n
    t = (pref // quantum) * quantum
    while t > quantum and n % t:
        t -= quantum
    assert n % t == 0, (n, pref)
    return t


def _rms(x, gain):
    ms = jnp.mean(x * x, axis=-1, keepdims=True)
    return x * lax.rsqrt(ms + RMS_EPS) * gain


def _sigmoid(x):
    return 1.0 / (1.0 + jnp.exp(-x))


def _rmsnorm_body(x_ref, g_ref, o_ref):
    o_ref[...] = _rms(x_ref[...], g_ref[...]).astype(o_ref.dtype)


def rmsnorm(x, gain, rows=256):
    s, d = x.shape
    tr = _tile(s, rows, 8)
    return pl.pallas_call(
        _rmsnorm_body,
        grid=(s // tr,),
        in_specs=[pl.BlockSpec((tr, d), lambda i: (i, 0)),
                  pl.BlockSpec((1, d), lambda i: (0, 0))],
        out_specs=pl.BlockSpec((tr, d), lambda i: (i, 0)),
        out_shape=jax.ShapeDtypeStruct((s, d), BF16),
        compiler_params=_cparams(("parallel",)),
        name="rmsnorm",
    )(x, gain.reshape(1, d).astype(F32))


def _mm_body(*refs, n_rhs, nk, epilogue, has_res, grouped):
    pos = 0
    if grouped:
        nused_ref = refs[1]
        pos = 2
    a_ref = refs[pos]
    w_refs = refs[pos + 1:pos + 1 + n_rhs]
    pos += 1 + n_rhs
    res_ref = refs[pos] if has_res else None
    pos += int(has_res)
    o_ref = refs[pos]
    acc_refs = refs[pos + 1:]
    k = pl.program_id(2)

    def finish(parts):
        if epilogue == "swiglu":
            y = parts[0] * _sigmoid(parts[0]) * parts[1]
        elif epilogue == "glu":
            y = parts[0] * _sigmoid(parts[1])
        else:
            y = parts[0]
        if has_res:
            y = res_ref[...] + y
        o_ref[...] = y.astype(o_ref.dtype)

    def compute():
        a = a_ref[...]
        parts = [jnp.dot(a, w[...], preferred_element_type=F32) for w in w_refs]
        if nk == 1:
            finish(parts)
            return

        @pl.when(k == 0)
        def _():
            for acc, p in zip(acc_refs, parts):
                acc[...] = p

        @pl.when(k > 0)
        def _():
            for acc, p in zip(acc_refs, parts):
                acc[...] += p

        @pl.when(k == nk - 1)
        def _():
            finish([acc[...] for acc in acc_refs])

    if grouped:
        valid = pl.program_id(0) < nused_ref[0]
        pl.when(valid)(compute)

        @pl.when(jnp.logical_not(valid) & (k == nk - 1))
        def _():
            o_ref[...] = jnp.zeros(o_ref.shape, o_ref.dtype)
    else:
        compute()


def matmul(a, ws, *, res=None, epilogue=None, out_dtype=BF16, tm=1024, tn=1024, tk=4096,
           w_col_offsets=None, n_out=None, tile_expert=None, n_used=None, name="matmul"):
    m, kdim = a.shape
    grouped = tile_expert is not None
    n_total = ws[0].shape[-1]
    n = n_total if n_out is None else n_out
    offs = [0] * len(ws) if w_col_offsets is None else w_col_offsets
    tm, tn, tk = _tile(m, tm), _tile(n, tn), _tile(kdim, tk)
    nk = kdim // tk
    grid = (m // tm, n // tn, nk)
    assert all(o % tn == 0 for o in offs)

    if grouped:
        def a_map(i, j, k, te, nu):
            return (i, jnp.where(i < nu[0], k, 0))

        def w_map(off):
            def f(i, j, k, te, nu):
                v = i < nu[0]
                return (te[i], jnp.where(v, k, 0), jnp.where(v, j, 0) + off // tn)
            return f

        def o_map(i, j, k, te, nu):
            return (i, j)
        w_specs = [pl.BlockSpec((None, tk, tn), w_map(o)) for o in offs]
    else:
        def a_map(i, j, k):
            return (i, k)

        def w_map(off):
            return lambda i, j, k: (k, j + off // tn)

        def o_map(i, j, k):
            return (i, j)
        w_specs = [pl.BlockSpec((tk, tn), w_map(o)) for o in offs]

    in_specs = [pl.BlockSpec((tm, tk), a_map)] + w_specs
    args = [a] + list(ws)
    if res is not None:
        in_specs.append(pl.BlockSpec((tm, tn), o_map))
        args.append(res)
    scratch = [pltpu.VMEM((tm, tn), F32) for _ in ws] if nk > 1 else []
    body = functools.partial(_mm_body, n_rhs=len(ws), nk=nk, epilogue=epilogue,
                             has_res=res is not None, grouped=grouped)
    out_shape = jax.ShapeDtypeStruct((m, n), out_dtype)
    sem = ("parallel", "parallel", "arbitrary")
    if grouped:
        grid_spec = pltpu.PrefetchScalarGridSpec(
            num_scalar_prefetch=2, grid=grid, in_specs=in_specs,
            out_specs=pl.BlockSpec((tm, tn), o_map), scratch_shapes=scratch)
        return pl.pallas_call(body, grid_spec=grid_spec, out_shape=out_shape,
                              compiler_params=_cparams(sem), name=name)(
                                  tile_expert, n_used, *args)
    return pl.pallas_call(body, grid=grid, in_specs=in_specs,
                          out_specs=pl.BlockSpec((tm, tn), o_map), out_shape=out_shape,
                          scratch_shapes=scratch, compiler_params=_cparams(sem),
                          name=name)(*args)


def _attn_body(sink_ref, q_ref, kc_ref, kp_ref, vc_ref, vp_ref, qg_ref, kg_ref, o_ref, *, tq):
    i = pl.program_id(0)
    g = pl.program_id(1)
    blk = ATTN_BLOCK
    scale = HEAD_DIM ** -0.5
    qg = qg_ref[...]
    kg = kg_ref[...]
    k_all = jnp.concatenate([_rms(kp_ref[...].astype(F32), kg).astype(BF16),
                             _rms(kc_ref[...].astype(F32), kg).astype(BF16)], axis=0)
    v_all = jnp.concatenate([vp_ref[...], vc_ref[...]], axis=0)

    rows = Q_PER_KV * blk
    r = lax.broadcasted_iota(jnp.int32, (rows, 2 * blk), 0)
    c = lax.broadcasted_iota(jnp.int32, (rows, 2 * blk), 1)
    diff = (r % blk) + blk - c
    band = (diff >= 0) & (diff < blk)
    head = lax.broadcasted_iota(jnp.int32, (rows, 1), 0) // blk
    sink = jnp.zeros((rows, 1), F32)
    for a in range(Q_PER_KV):
        sink = jnp.where(head == a, sink_ref[g * Q_PER_KV + a], sink)

    for j in range(tq // blk):
        k2 = k_all[j * blk:(j + 2) * blk]
        v2 = v_all[j * blk:(j + 2) * blk]
        q = jnp.concatenate(
            [q_ref[j * blk:(j + 1) * blk, a * HEAD_DIM:(a + 1) * HEAD_DIM]
             for a in range(Q_PER_KV)], axis=0)
        qn = _rms(q.astype(F32), qg).astype(BF16)
        s = lax.dot_general(qn, k2, (((1,), (1,)), ((), ())),
                            preferred_element_type=F32) * scale
        if j == 0:
            valid = band & (c >= jnp.where(i > 0, 0, blk))
        else:
            valid = band
        s = jnp.where(valid, s, NEG_INF)
        m = jnp.maximum(jnp.max(s, axis=-1, keepdims=True), sink)
        e = jnp.exp(s - m)
        denom = jnp.sum(e, axis=-1, keepdims=True) + jnp.exp(sink - m)
        p = (e / denom).astype(BF16)
        o = jnp.dot(p, v2, preferred_element_type=F32)
        for a in range(Q_PER_KV):
            o_ref[j * blk:(j + 1) * blk, a * HEAD_DIM:(a + 1) * HEAD_DIM] = (
                o[a * blk:(a + 1) * blk].astype(o_ref.dtype))


def attention(qkv, q_gain, k_gain, sinks, n_q_heads, tq=512):
    s = qkv.shape[0]
    n_kv = n_q_heads // Q_PER_KV
    tq = _tile(s, tq, ATTN_BLOCK)
    sub = tq // ATTN_BLOCK
    qw = Q_PER_KV * HEAD_DIM
    k0 = n_q_heads
    v0 = n_q_heads + n_kv
    grid_spec = pltpu.PrefetchScalarGridSpec(
        num_scalar_prefetch=1,
        grid=(s // tq, n_kv),
        in_specs=[
            pl.BlockSpec((tq, qw), lambda i, g, sk: (i, g)),
            pl.BlockSpec((tq, HEAD_DIM), lambda i, g, sk: (i, k0 + g)),
            pl.BlockSpec((ATTN_BLOCK, HEAD_DIM),
                         lambda i, g, sk: (jnp.maximum(i * sub - 1, 0), k0 + g)),
            pl.BlockSpec((tq, HEAD_DIM), lambda i, g, sk: (i, v0 + g)),
            pl.BlockSpec((ATTN_BLOCK, HEAD_DIM),
                         lambda i, g, sk: (jnp.maximum(i * sub - 1, 0), v0 + g)),
            pl.BlockSpec((1, HEAD_DIM), lambda i, g, sk: (0, 0)),
            pl.BlockSpec((1, HEAD_DIM), lambda i, g, sk: (0, 0)),
        ],
        out_specs=pl.BlockSpec((tq, qw), lambda i, g, sk: (i, g)),
    )
    return pl.pallas_call(
        functools.partial(_attn_body, tq=tq),
        grid_spec=grid_spec,
        out_shape=jax.ShapeDtypeStruct((s, n_q_heads * HEAD_DIM), BF16),
        compiler_params=_cparams(("parallel", "parallel")),
        name="swa_attention",
    )(sinks.astype(F32), qkv, qkv, qkv, qkv, qkv,
      q_gain.reshape(1, HEAD_DIM).astype(F32), k_gain.reshape(1, HEAD_DIM).astype(F32))


def _s5_body(u_ref, prm_ref, dsk_ref, y_ref):
    nc = u_ref.shape[0]
    p2 = 2 * SSM_STATE
    lc = SSM_CHUNK * SSM_GROUP
    prm = prm_ref[...]
    bt = prm[0:16]
    cc = prm[16:32]
    lam_re = prm[32:33]
    lam_im = prm[33:34]
    step = jnp.exp(prm[34:35])
    lane = lax.broadcasted_iota(jnp.int32, (1, p2), 1)
    sgn = jnp.where(lane < SSM_STATE, -1.0, 1.0).astype(F32)

    def swap(x):
        return pltpu.roll(x, SSM_STATE, 1)

    def powers(expo):
        mag = jnp.exp(expo * (lam_re * step))
        ang = expo * (lam_im * step)
        return mag * jnp.cos(ang), mag * jnp.sin(ang)

    row = lax.broadcasted_iota(jnp.int32, (48, 1), 0)
    expo = jnp.where(row < 32, row, 16 * (1 << jnp.maximum(row - 32, 0))).astype(F32)
    pr, pi = powers(expo)

    lbr, lbi = pr[1:2], pi[1:2]
    inv = 1.0 / (lam_re * lam_re + lam_im * lam_im)
    cf_re = ((lbr - 1.0) * lam_re + lbi * lam_im) * inv
    cf_im = (lbi * lam_re - (lbr - 1.0) * lam_im) * inv
    bb1 = cf_re * bt + sgn * cf_im * swap(bt)
    bb2 = swap(bb1)
    ca = jnp.where(lane < SSM_STATE, cc, -cc)
    cb = -swap(cc)

    def rep_rows(x):
        return jnp.broadcast_to(x[:, None, :], (16, SSM_GROUP, p2)).reshape(lc, p2)

    def tile_rows(x):
        return jnp.broadcast_to(x[None, :, :], (SSM_CHUNK, 16, p2)).reshape(lc, p2)

    rrow = lax.broadcasted_iota(jnp.int32, (16, 1), 0)
    qr, qi = powers((SSM_CHUNK - 1 - rrow).astype(F32))
    s_mat = (tile_rows(bb1) * rep_rows(qr) + tile_rows(bb2) * rep_rows(sgn * qi)).astype(BF16)
    c_t = (tile_rows(ca) * rep_rows(pr[1:17]) + tile_rows(cb) * rep_rows(pi[1:17])).astype(BF16)
    f_t = tile_rows(ca) * rep_rows(pr[0:16]) + tile_rows(cb) * rep_rows(pi[0:16])
    kern = lax.dot_general(bb1, f_t, (((1,), (1,)), ((), ())),
                           precision=HIGHEST, preferred_element_type=F32)
    lane_lc = lax.broadcasted_iota(jnp.int32, (16, lc), 1)
    blocks = [kern]
    for s in range(1, SSM_CHUNK):
        shifted = pltpu.roll(kern, s * SSM_GROUP, 1)
        blocks.append(jnp.where(lane_lc >= s * SSM_GROUP, shifted, 0.0))
    toep = jnp.concatenate(blocks, axis=0).astype(BF16)

    u = u_ref[...]
    y = jnp.dot(u, toep, preferred_element_type=F32)
    x = jnp.dot(u, s_mat, preferred_element_type=F32)

    rown = lax.broadcasted_iota(jnp.int32, (nc, 1), 0)
    d, j = 1, 0
    while d < nc:
        ar = pr[32 + j:33 + j]
        ai = sgn * pi[32 + j:33 + j]
        sh = jnp.where(rown >= d, pltpu.roll(x, d, 0), 0.0)
        x = x + ar * sh + ai * swap(sh)
        d, j = d * 2, j + 1
    h0 = jnp.where(rown >= 1, pltpu.roll(x, 1, 0), 0.0)
    y = y + lax.dot_general(h0.astype(BF16), c_t, (((1,), (1,)), ((), ())),
                            preferred_element_type=F32)
    z = y + dsk_ref[...] * u.astype(F32)
    gelu = 0.5 * z * (1.0 + jnp.tanh(0.7978845608028654 * (z + 0.044715 * (z * z * z))))
    y_ref[...] = gelu.astype(y_ref.dtype)


def s5_mix(u_t, prm, dsk):
    g, nc, lc = u_t.shape
    return pl.pallas_call(
        _s5_body,
        grid=(g,),
        in_specs=[pl.BlockSpec((None, nc, lc), lambda i: (i, 0, 0)),
                  pl.BlockSpec((None, 40, 2 * SSM_STATE), lambda i: (i, 0, 0)),
                  pl.BlockSpec((None, 1, lc), lambda i: (i, 0, 0))],
        out_specs=pl.BlockSpec((None, nc, lc), lambda i: (i, 0, 0)),
        out_shape=jax.ShapeDtypeStruct((g, nc, lc), BF16),
        compiler_params=_cparams(("parallel",)),
        name="s5_mix",
    )(u_t, prm, dsk)


def _router_body(x_ref, g_ref, wr_ref, h_ref, info_ref, cnt_ref, carry_ref, *, n_exp):
    i = pl.program_id(0)
    tr = x_ref.shape[0]

    @pl.when(i == 0)
    def _():
        carry_ref[...] = jnp.zeros(carry_ref.shape, F32)

    h = _rms(x_ref[...], g_ref[...])
    h_ref[...] = h.astype(h_ref.dtype)
    logits = jnp.dot(h, wr_ref[...], precision=HIGHEST, preferred_element_type=F32)
    lane = lax.broadcasted_iota(jnp.int32, (tr, LANES), 1).astype(F32)
    neg = jnp.float32(-jnp.inf)
    l1 = jnp.where(lane < n_exp, logits, neg)
    m1 = jnp.max(l1, axis=-1, keepdims=True)
    i1 = jnp.min(jnp.where(l1 == m1, lane, float(LANES)), axis=-1, keepdims=True)
    l2 = jnp.where(lane == i1, neg, l1)
    m2 = jnp.max(l2, axis=-1, keepdims=True)
    i2 = jnp.min(jnp.where(l2 == m2, lane, float(LANES)), axis=-1, keepdims=True)
    e = jnp.exp(m2 - m1)
    w1 = 1.0 / (1.0 + e)
    w2 = e / (1.0 + e)

    hit1 = lane == i1
    hit2 = lane == i2
    onehot = jnp.where(hit1 | hit2, 1.0, 0.0)
    rr = lax.broadcasted_iota(jnp.int32, (tr, tr), 0)
    cc = lax.broadcasted_iota(jnp.int32, (tr, tr), 1)
    lower = jnp.where(rr > cc, 1.0, 0.0).astype(BF16)
    before = jnp.dot(lower, onehot.astype(BF16), preferred_element_type=F32) + carry_ref[0:1]
    r1 = jnp.sum(jnp.where(hit1, before, 0.0), axis=-1, keepdims=True)
    r2 = jnp.sum(jnp.where(hit2, before, 0.0), axis=-1, keepdims=True)
    total = carry_ref[0:1] + jnp.sum(onehot, axis=0, keepdims=True)
    carry_ref[...] = jnp.broadcast_to(total, carry_ref.shape)
    cnt_ref[...] = jnp.broadcast_to(total, cnt_ref.shape)

    info = jnp.where(lane == 0, i1, 0.0)
    for idx, val in ((1, i2), (2, r1), (3, r2), (4, w1), (5, w2)):
        info = jnp.where(lane == idx, val, info)
    info_ref[...] = info


def router(x, gain, w_router, rows=256):
    s, d = x.shape
    n_exp = w_router.shape[1]
    tr = _tile(s, rows, 8)
    wr = jnp.zeros((d, LANES), F32).at[:, :n_exp].set(w_router.astype(F32))
    return pl.pallas_call(
        functools.partial(_router_body, n_exp=n_exp),
        grid=(s // tr,),
        in_specs=[pl.BlockSpec((tr, d), lambda i: (i, 0)),
                  pl.BlockSpec((1, d), lambda i: (0, 0)),
                  pl.BlockSpec((d, LANES), lambda i: (0, 0))],
        out_specs=[pl.BlockSpec((tr, d), lambda i: (i, 0)),
                   pl.BlockSpec((tr, LANES), lambda i: (i, 0)),
                   pl.BlockSpec((8, LANES), lambda i: (0, 0))],
        out_shape=[jax.ShapeDtypeStruct((s, d), BF16),
                   jax.ShapeDtypeStruct((s, LANES), F32),
                   jax.ShapeDtypeStruct((8, LANES), F32)],
        scratch_shapes=[pltpu.VMEM((8, LANES), F32)],
        compiler_params=_cparams(("arbitrary",)),
        name="moe_router",
    )(x, gain.reshape(1, d).astype(F32), wr)


def _dispatch_body(rows_ref, h_ref, xg_in_ref, xg_ref, sem, *, tt):
    del xg_in_ref
    base = pl.program_id(0) * tt

    def copy(n):
        return pltpu.make_async_copy(h_ref.at[base + n // TOP_K], xg_ref.at[rows_ref[base * TOP_K + n]], sem)

    def start(n, carry):
        copy(n).start()
        return carry

    def wait(n, carry):
        copy(n).wait()
        return carry

    lax.fori_loop(0, tt * TOP_K, start, 0)
    lax.fori_loop(0, tt * TOP_K, wait, 0)


def dispatch(h3, rows, n_slots, tokens=256):
    s = h3.shape[0]
    tt = _tile(s, tokens, 8)
    xg0 = jnp.zeros((n_slots,) + h3.shape[1:], h3.dtype)
    grid_spec = pltpu.PrefetchScalarGridSpec(
        num_scalar_prefetch=1,
        grid=(s // tt,),
        in_specs=[pl.BlockSpec(memory_space=pl.ANY), pl.BlockSpec(memory_space=pl.ANY)],
        out_specs=pl.BlockSpec(memory_space=pl.ANY),
        scratch_shapes=[pltpu.SemaphoreType.DMA(())],
    )
    return pl.pallas_call(
        functools.partial(_dispatch_body, tt=tt),
        grid_spec=grid_spec,
        out_shape=jax.ShapeDtypeStruct(xg0.shape, xg0.dtype),
        input_output_aliases={2: 0},
        compiler_params=_cparams(("arbitrary",)),
        name="moe_dispatch",
    )(rows, h3, xg0)


def _combine_body(rows_ref, x_ref, w_ref, y_ref, o_ref, buf, sem, *, tt):
    base = pl.program_id(0) * tt

    def copy(n):
        return pltpu.make_async_copy(y_ref.at[rows_ref[base * TOP_K + n]],
                                     buf.at[n % TOP_K, n // TOP_K], sem)

    def start(n, carry):
        copy(n).start()
        return carry

    def wait(n, carry):
        copy(n).wait()
        return carry

    lax.fori_loop(0, tt * TOP_K, start, 0)
    lax.fori_loop(0, tt * TOP_K, wait, 0)
    w = w_ref[...]
    w0 = w[:, :, 4:5]
    w1 = w[:, :, 5:6]
    o_ref[...] = (x_ref[...] + w0 * buf[0].astype(F32) + w1 * buf[1].astype(F32))


def combine(x3, info, y3, rows, tokens=256):
    s, nsub, _ = x3.shape
    tt = _tile(s, tokens, 8)
    grid_spec = pltpu.PrefetchScalarGridSpec(
        num_scalar_prefetch=1,
        grid=(s // tt,),
        in_specs=[pl.BlockSpec((tt, nsub, LANES), lambda i, r: (i, 0, 0)),
                  pl.BlockSpec((tt, 1, LANES), lambda i, r: (i, 0, 0)),
                  pl.BlockSpec(memory_space=pl.ANY)],
        out_specs=pl.BlockSpec((tt, nsub, LANES), lambda i, r: (i, 0, 0)),
        scratch_shapes=[pltpu.VMEM((TOP_K, tt, nsub, LANES), y3.dtype),
                        pltpu.SemaphoreType.DMA(())],
    )
    return pl.pallas_call(
        functools.partial(_combine_body, tt=tt),
        grid_spec=grid_spec,
        out_shape=jax.ShapeDtypeStruct(x3.shape, F32),
        compiler_params=_cparams(("arbitrary",)),
        name="moe_combine",
    )(rows, x3, info.reshape(s, 1, LANES), y3)


def _attention_layer(x, attn_norm, w_qkv, q_norm, k_norm, sinks, w_o):
    n_q_heads = w_o.shape[0] // HEAD_DIM
    h = rmsnorm(x, attn_norm)
    qkv = matmul(h, [w_qkv.astype(BF16)], name="qkv_proj")
    o = attention(qkv, q_norm, k_norm, sinks, n_q_heads)
    return matmul(o, [w_o.astype(BF16)], res=x, out_dtype=F32, name="attn_out_proj")


def _ffn_layer(x, ffn_norm, w_gate, w_up, w_down):
    h = rmsnorm(x, ffn_norm)
    act = matmul(h, [w_gate.astype(BF16), w_up.astype(BF16)], epilogue="swiglu", tn=512,
                 name="ffn_gate_up")
    return matmul(act, [w_down.astype(BF16)], res=x, out_dtype=F32, tk=2048, name="ffn_down")


def _s5_layer(x, ssm_norm, lam_re, lam_im, log_step, b_re, b_im, c_re, c_im, d_skip, w_glu):
    s, d = x.shape
    g = d // SSM_GROUP
    nc = s // SSM_CHUNK
    lc = SSM_CHUNK * SSM_GROUP
    h = rmsnorm(x, ssm_norm)
    u_t = h.reshape(nc, SSM_CHUNK, g, SSM_GROUP).transpose(2, 0, 1, 3).reshape(g, nc, lc)

    def two(a, b):
        return jnp.concatenate([a, b], axis=-1).astype(F32)
    prm = jnp.concatenate([
        two(b_re.transpose(0, 2, 1), b_im.transpose(0, 2, 1)),
        two(c_re, c_im),
        two(lam_re, lam_re)[:, None, :],
        two(lam_im, lam_im)[:, None, :],
        jnp.broadcast_to(log_step.astype(F32)[:, None, None], (g, 6, 2 * SSM_STATE)),
    ], axis=1)
    dsk = jnp.tile(d_skip.astype(F32).reshape(g, 1, SSM_GROUP), (1, 1, SSM_CHUNK))
    y_t = s5_mix(u_t, prm, dsk)
    y = y_t.reshape(g, nc, SSM_CHUNK, SSM_GROUP).transpose(1, 2, 0, 3).reshape(s, d)
    wg = w_glu.astype(BF16)
    return matmul(y, [wg, wg], epilogue="glu", res=x, out_dtype=F32, tn=512,
                  w_col_offsets=[0, d], n_out=d, name="s5_glu")


def _moe_layer(x, moe_norm, w_router, w_gate_e, w_up_e, w_down_e, tm=512):
    s, d = x.shape
    n_exp = w_router.shape[1]
    nsub = d // LANES
    h, info, cnt = router(x, moe_norm, w_router)

    counts = cnt[0, :n_exp].astype(jnp.int32)
    padded = ((counts + tm - 1) // tm) * tm
    ends = jnp.cumsum(padded)
    starts = ends - padded
    n_slots = s * TOP_K + n_exp * tm
    n_tiles = n_slots // tm
    experts = info[:, 0:TOP_K].astype(jnp.int32)
    ranks = info[:, TOP_K:2 * TOP_K].astype(jnp.int32)
    rows = (starts[experts] + ranks).reshape(s * TOP_K)
    n_used = (ends[-1] // tm).astype(jnp.int32).reshape(1)
    tile_start = jnp.minimum(jnp.arange(n_tiles, dtype=jnp.int32), n_used[0] - 1) * tm
    tile_expert = jnp.sum(tile_start[:, None] >= ends[None, :], axis=1).astype(jnp.int32)

    xg = dispatch(h.reshape(s, nsub, LANES), rows, n_slots).reshape(n_slots, d)
    act = matmul(xg, [w_gate_e.astype(BF16), w_up_e.astype(BF16)], epilogue="swiglu",
                 tm=tm, tn=512, tile_expert=tile_expert, n_used=n_used, name="moe_gate_up")
    yg = matmul(act, [w_down_e.astype(BF16)], tm=tm, tn=1024, tk=2048,
                tile_expert=tile_expert, n_used=n_used, name="moe_down")
    out3 = combine(x.reshape(s, nsub, LANES), info, yg.reshape(n_slots, nsub, LANES), rows)
    return out3.reshape(s, d)


def kernel(x, attn_norm, w_qkv, q_norm, k_norm, sinks, w_o, ffn_norm, w_gate, w_up, w_down,
           ssm_norm, lam_re, lam_im, log_step, b_re, b_im, c_re, c_im, d_skip, w_glu,
           moe_norm, w_router, w_gate_e, w_up_e, w_down_e):
    b, s, d = x.shape
    depth = attn_norm.shape[0] + ssm_norm.shape[0]
    outs = []
    for bi in range(b):
        xb = x[bi]
        for i in range(depth):
            j = i // 2
            if i % 2 == 0:
                xb = _attention_layer(xb, attn_norm[j], w_qkv[j], q_norm[j], k_norm[j],
                                      sinks[j], w_o[j])
                xb = _ffn_layer(xb, ffn_norm[j], w_gate[j], w_up[j], w_down[j])
            else:
                xb = _s5_layer(xb, ssm_norm[j], lam_re[j], lam_im[j], log_step[j], b_re[j],
                               b_im[j], c_re[j], c_im[j], d_skip[j], w_glu[j])
                xb = _moe_layer(xb, moe_norm[j], w_router[j], w_gate_e[j], w_up_e[j],
                                w_down_e[j])
        outs.append(xb)
    return jnp.stack(outs, axis=0)
```

```python
import functools

import jax
import jax.numpy as jnp
from jax import lax
from jax.experimental import pallas as pl
from jax.experimental.pallas import tpu as pltpu

F32 = jnp.float32
BF16 = jnp.bfloat16
HIGHEST = lax.Precision.HIGHEST

RMS_EPS = 1e-5
NEG_INF = -1e30

HEAD_DIM = 128
Q_PER_KV = 4
ATTN_BLOCK = 128
SSM_GROUP = 16
SSM_STATE = 64
SSM_CHUNK = 16
TOP_K = 2

LANES = 128
VMEM_LIMIT_MB = 56


def _cparams(semantics, vmem_mb=VMEM_LIMIT_MB):
    return pltpu.CompilerParams(dimension_semantics=semantics,
                                vmem_limit_bytes=vmem_mb << 20)


def _tile(n, pref, quantum=128):
    if n <= pref:
        return n
    t = (pref // quantum) * quantum
    while t > quantum and n % t:
        t -= quantum
    assert n % t == 0, (n, pref)
    return t


def _rms(x, gain):
    ms = jnp.mean(x * x, axis=-1, keepdims=True)
    return x * lax.rsqrt(ms + RMS_EPS) * gain


def _sigmoid(x):
    return 1.0 / (1.0 + jnp.exp(-x))


def _rmsnorm_body(x_ref, g_ref, o_ref):
    o_ref[...] = _rms(x_ref[...], g_ref[...]).astype(o_ref.dtype)


def rmsnorm(x, gain, rows=256):
    s, d = x.shape
    tr = _tile(s, rows, 8)
    return pl.pallas_call(
        _rmsnorm_body,
        grid=(s // tr,),
        in_specs=[pl.BlockSpec((tr, d), lambda i: (i, 0)),
                  pl.BlockSpec((1, d), lambda i: (0, 0))],
        out_specs=pl.BlockSpec((tr, d), lambda i: (i, 0)),
        out_shape=jax.ShapeDtypeStruct((s, d), BF16),
        compiler_params=_cparams(("parallel",)),
        name="rmsnorm",
    )(x, gain.reshape(1, d).astype(F32))


def _mm_body(*refs, n_rhs, nk, epilogue, has_res, grouped):
    pos = 0
    if grouped:
        nused_ref = refs[1]
        pos = 2
    a_ref = refs[pos]
    w_refs = refs[pos + 1:pos + 1 + n_rhs]
    pos += 1 + n_rhs
    res_ref = refs[pos] if has_res else None
    pos += int(has_res)
    o_ref = refs[pos]
    acc_refs = refs[pos + 1:]
    k = pl.program_id(2)

    def finish(parts):
        if epilogue == "swiglu":
            y = parts[0] * _sigmoid(parts[0]) * parts[1]
        elif epilogue == "glu":
            y = parts[0] * _sigmoid(parts[1])
        else:
            y = parts[0]
        if has_res:
            y = res_ref[...] + y
        o_ref[...] = y.astype(o_ref.dtype)

    def compute():
        a = a_ref[...].astype(BF16)
        parts = [jnp.dot(a, w[...], preferred_element_type=F32) for w in w_refs]
        if nk == 1:
            finish(parts)
            return

        @pl.when(k == 0)
        def _():
            for acc, p in zip(acc_refs, parts):
                acc[...] = p

        @pl.when(k > 0)
        def _():
            for acc, p in zip(acc_refs, parts):
                acc[...] += p

        @pl.when(k == nk - 1)
        def _():
            finish([acc[...] for acc in acc_refs])

    if grouped:
        valid = pl.program_id(0) < nused_ref[0]
        pl.when(valid)(compute)

        @pl.when(jnp.logical_not(valid) & (k == nk - 1))
        def _():
            o_ref[...] = jnp.zeros(o_ref.shape, o_ref.dtype)
    else:
        compute()


def matmul(a, ws, *, res=None, epilogue=None, out_dtype=BF16, tm=1024, tn=1024, tk=4096,
           w_col_offsets=None, n_out=None, tile_expert=None, n_used=None, name="matmul"):
    m, kdim = a.shape
    grouped = tile_expert is not None
    n_total = ws[0].shape[-1]
    n = n_total if n_out is None else n_out
    offs = [0] * len(ws) if w_col_offsets is None else w_col_offsets
    tm, tn, tk = _tile(m, tm), _tile(n, tn), _tile(kdim, tk)
    nk = kdim // tk
    grid = (m // tm, n // tn, nk)
    assert all(o % tn == 0 for o in offs)

    if grouped:
        def a_map(i, j, k, te, nu):
            v = i < nu[0]
            return (jnp.where(v, i, nu[0] - 1), jnp.where(v, k, 0))

        def w_map(off):
            def f(i, j, k, te, nu):
                v = i < nu[0]
                return (te[i], jnp.where(v, k, 0), jnp.where(v, j, 0) + off // tn)
            return f

        def o_map(i, j, k, te, nu):
            return (i, j)
        w_specs = [pl.BlockSpec((None, tk, tn), w_map(o)) for o in offs]
    else:
        def a_map(i, j, k):
            return (i, k)

        def w_map(off):
            return lambda i, j, k: (k, j + off // tn)

        def o_map(i, j, k):
            return (i, j)
        w_specs = [pl.BlockSpec((tk, tn), w_map(o)) for o in offs]

    in_specs = [pl.BlockSpec((tm, tk), a_map)] + w_specs
    args = [a] + list(ws)
    if res is not None:
        in_specs.append(pl.BlockSpec((tm, tn), o_map))
        args.append(res)
    scratch = [pltpu.VMEM((tm, tn), F32) for _ in ws] if nk > 1 else []
    body = functools.partial(_mm_body, n_rhs=len(ws), nk=nk, epilogue=epilogue,
                             has_res=res is not None, grouped=grouped)
    out_shape = jax.ShapeDtypeStruct((m, n), out_dtype)
    sem = ("parallel", "parallel", "arbitrary")
    if grouped:
        grid_spec = pltpu.PrefetchScalarGridSpec(
            num_scalar_prefetch=2, grid=grid, in_specs=in_specs,
            out_specs=pl.BlockSpec((tm, tn), o_map), scratch_shapes=scratch)
        return pl.pallas_call(body, grid_spec=grid_spec, out_shape=out_shape,
                              compiler_params=_cparams(sem), name=name)(
                                  tile_expert, n_used, *args)
    return pl.pallas_call(body, grid=grid, in_specs=in_specs,
                          out_specs=pl.BlockSpec((tm, tn), o_map), out_shape=out_shape,
                          scratch_shapes=scratch, compiler_params=_cparams(sem),
                          name=name)(*args)


def _attn_body(sink_ref, q_ref, kc_ref, kp_ref, vc_ref, vp_ref, qg_ref, kg_ref, o_ref, *, tq):
    i = pl.program_id(0)
    g = pl.program_id(1)
    blk = ATTN_BLOCK
    scale = HEAD_DIM ** -0.5
    qg = qg_ref[...]
    kg = kg_ref[...]
    k_all = jnp.concatenate([_rms(kp_ref[...].astype(F32), kg).astype(BF16),
                             _rms(kc_ref[...].astype(F32), kg).astype(BF16)], axis=0)
    v_all = jnp.concatenate([vp_ref[...], vc_ref[...]], axis=0)

    rows = Q_PER_KV * blk
    r = lax.broadcasted_iota(jnp.int32, (rows, 2 * blk), 0)
    c = lax.broadcasted_iota(jnp.int32, (rows, 2 * blk), 1)
    diff = (r % blk) + blk - c
    band = (diff >= 0) & (diff < blk)
    head = lax.broadcasted_iota(jnp.int32, (rows, 1), 0) // blk
    sink = jnp.zeros((rows, 1), F32)
    for a in range(Q_PER_KV):
        sink = jnp.where(head == a, sink_ref[g * Q_PER_KV + a], sink)

    for j in range(tq // blk):
        k2 = k_all[j * blk:(j + 2) * blk]
        v2 = v_all[j * blk:(j + 2) * blk]
        q = jnp.concatenate(
            [q_ref[j * blk:(j + 1) * blk, a * HEAD_DIM:(a + 1) * HEAD_DIM]
             for a in range(Q_PER_KV)], axis=0)
        qn = _rms(q.astype(F32), qg).astype(BF16)
        s = lax.dot_general(qn, k2, (((1,), (1,)), ((), ())),
                            preferred_element_type=F32) * scale
        if j == 0:
            valid = band & (c >= jnp.where(i > 0, 0, blk))
        else:
            valid = band
        s = jnp.where(valid, s, NEG_INF)
        m = jnp.maximum(jnp.max(s, axis=-1, keepdims=True), sink)
        e = jnp.exp(s - m)
        denom = jnp.sum(e, axis=-1, keepdims=True) + jnp.exp(sink - m)
        p = (e / denom).astype(BF16)
        o = jnp.dot(p, v2, preferred_element_type=F32)
        for a in range(Q_PER_KV):
            o_ref[j * blk:(j + 1) * blk, a * HEAD_DIM:(a + 1) * HEAD_DIM] = (
                o[a * blk:(a + 1) * blk].astype(o_ref.dtype))


def attention(qkv, q_gain, k_gain, sinks, n_q_heads, tq=512):
    s = qkv.shape[0]
    n_kv = n_q_heads // Q_PER_KV
    tq = _tile(s, tq, ATTN_BLOCK)
    sub = tq // ATTN_BLOCK
    qw = Q_PER_KV * HEAD_DIM
    k0 = n_q_heads
    v0 = n_q_heads + n_kv
    grid_spec = pltpu.PrefetchScalarGridSpec(
        num_scalar_prefetch=1,
        grid=(s // tq, n_kv),
        in_specs=[
            pl.BlockSpec((tq, qw), lambda i, g, sk: (i, g)),
            pl.BlockSpec((tq, HEAD_DIM), lambda i, g, sk: (i, k0 + g)),
            pl.BlockSpec((ATTN_BLOCK, HEAD_DIM),
                         lambda i, g, sk: (jnp.maximum(i * sub - 1, 0), k0 + g)),
            pl.BlockSpec((tq, HEAD_DIM), lambda i, g, sk: (i, v0 + g)),
            pl.BlockSpec((ATTN_BLOCK, HEAD_DIM),
                         lambda i, g, sk: (jnp.maximum(i * sub - 1, 0), v0 + g)),
            pl.BlockSpec((1, HEAD_DIM), lambda i, g, sk: (0, 0)),
            pl.BlockSpec((1, HEAD_DIM), lambda i, g, sk: (0, 0)),
        ],
        out_specs=pl.BlockSpec((tq, qw), lambda i, g, sk: (i, g)),
    )
    return pl.pallas_call(
        functools.partial(_attn_body, tq=tq),
        grid_spec=grid_spec,
        out_shape=jax.ShapeDtypeStruct((s, n_q_heads * HEAD_DIM), BF16),
        compiler_params=_cparams(("parallel", "parallel")),
        name="swa_attention",
    )(sinks.astype(F32), qkv, qkv, qkv, qkv, qkv,
      q_gain.reshape(1, HEAD_DIM).astype(F32), k_gain.reshape(1, HEAD_DIM).astype(F32))


def _s5_body(u_ref, prm_ref, dsk_ref, y_ref):
    nc = u_ref.shape[0]
    p2 = 2 * SSM_STATE
    lc = SSM_CHUNK * SSM_GROUP
    prm = prm_ref[...]
    bt = prm[0:16]
    cc = prm[16:32]
    lam_re = prm[32:33]
    lam_im = prm[33:34]
    step = jnp.exp(prm[34:35])
    lane = lax.broadcasted_iota(jnp.int32, (1, p2), 1)
    sgn = jnp.where(lane < SSM_STATE, -1.0, 1.0).astype(F32)

    def swap(x):
        return pltpu.roll(x, SSM_STATE, 1)

    def powers(expo):
        mag = jnp.exp(expo * (lam_re * step))
        ang = expo * (lam_im * step)
        return mag * jnp.cos(ang), mag * jnp.sin(ang)

    row = lax.broadcasted_iota(jnp.int32, (48, 1), 0)
    expo = jnp.where(row < 32, row, 16 * (1 << jnp.maximum(row - 32, 0))).astype(F32)
    pr, pi = powers(expo)

    lbr, lbi = pr[1:2], pi[1:2]
    inv = 1.0 / (lam_re * lam_re + lam_im * lam_im)
    cf_re = ((lbr - 1.0) * lam_re + lbi * lam_im) * inv
    cf_im = (lbi * lam_re - (lbr - 1.0) * lam_im) * inv
    bb1 = cf_re * bt + sgn * cf_im * swap(bt)
    bb2 = swap(bb1)
    ca = jnp.where(lane < SSM_STATE, cc, -cc)
    cb = -swap(cc)

    def rep_rows(x):
        return jnp.broadcast_to(x[:, None, :], (16, SSM_GROUP, p2)).reshape(lc, p2)

    def tile_rows(x):
        return jnp.broadcast_to(x[None, :, :], (SSM_CHUNK, 16, p2)).reshape(lc, p2)

    rrow = lax.broadcasted_iota(jnp.int32, (16, 1), 0)
    qr, qi = powers((SSM_CHUNK - 1 - rrow).astype(F32))
    s_mat = (tile_rows(bb1) * rep_rows(qr) + tile_rows(bb2) * rep_rows(sgn * qi)).astype(BF16)
    c_t = (tile_rows(ca) * rep_rows(pr[1:17]) + tile_rows(cb) * rep_rows(pi[1:17])).astype(BF16)
    f_t = tile_rows(ca) * rep_rows(pr[0:16]) + tile_rows(cb) * rep_rows(pi[0:16])
    kern = lax.dot_general(bb1, f_t, (((1,), (1,)), ((), ())),
                           precision=HIGHEST, preferred_element_type=F32)
    lane_lc = lax.broadcasted_iota(jnp.int32, (16, lc), 1)
    blocks = [kern]
    for s in range(1, SSM_CHUNK):
        shifted = pltpu.roll(kern, s * SSM_GROUP, 1)
        blocks.append(jnp.where(lane_lc >= s * SSM_GROUP, shifted, 0.0))
    toep = jnp.concatenate(blocks, axis=0).astype(BF16)

    u = u_ref[...]
    y = jnp.dot(u, toep, preferred_element_type=F32)
    x = jnp.dot(u, s_mat, preferred_element_type=F32)

    rown = lax.broadcasted_iota(jnp.int32, (nc, 1), 0)
    d, j = 1, 0
    while d < nc:
        ar = pr[32 + j:33 + j]
        ai = sgn * pi[32 + j:33 + j]
        sh = jnp.where(rown >= d, pltpu.roll(x, d, 0), 0.0)
        x = x + ar * sh + ai * swap(sh)
        d, j = d * 2, j + 1
    h0 = jnp.where(rown >= 1, pltpu.roll(x, 1, 0), 0.0)
    y = y + lax.dot_general(h0.astype(BF16), c_t, (((1,), (1,)), ((), ())),
                            preferred_element_type=F32)
    z = y + dsk_ref[...] * u.astype(F32)
    gelu = 0.5 * z * (1.0 + jnp.tanh(0.7978845608028654 * (z + 0.044715 * (z * z * z))))
    y_ref[...] = gelu.astype(y_ref.dtype)


def s5_mix(u_t, prm, dsk):
    g, nc, lc = u_t.shape
    return pl.pallas_call(
        _s5_body,
        grid=(g,),
        in_specs=[pl.BlockSpec((None, nc, lc), lambda i: (i, 0, 0)),
                  pl.BlockSpec((None, 40, 2 * SSM_STATE), lambda i: (i, 0, 0)),
                  pl.BlockSpec((None, 1, lc), lambda i: (i, 0, 0))],
        out_specs=pl.BlockSpec((None, nc, lc), lambda i: (i, 0, 0)),
        out_shape=jax.ShapeDtypeStruct((g, nc, lc), BF16),
        compiler_params=_cparams(("parallel",)),
        name="s5_mix",
    )(u_t, prm, dsk)


def _router_body(x_ref, g_ref, wr_ref, info_ref, cnt_ref, carry_ref, *, n_exp):
    i = pl.program_id(0)
    tr = x_ref.shape[0]

    @pl.when(i == 0)
    def _():
        carry_ref[...] = jnp.zeros(carry_ref.shape, F32)

    h = _rms(x_ref[...], g_ref[...])
    logits = jnp.dot(h, wr_ref[...], precision=HIGHEST, preferred_element_type=F32)
    lane = lax.broadcasted_iota(jnp.int32, (tr, LANES), 1).astype(F32)
    neg = jnp.float32(-jnp.inf)
    l1 = jnp.where(lane < n_exp, logits, neg)
    m1 = jnp.max(l1, axis=-1, keepdims=True)
    i1 = jnp.min(jnp.where(l1 == m1, lane, float(LANES)), axis=-1, keepdims=True)
    l2 = jnp.where(lane == i1, neg, l1)
    m2 = jnp.max(l2, axis=-1, keepdims=True)
    i2 = jnp.min(jnp.where(l2 == m2, lane, float(LANES)), axis=-1, keepdims=True)
    e = jnp.exp(m2 - m1)
    w1 = 1.0 / (1.0 + e)
    w2 = e / (1.0 + e)

    hit1 = lane == i1
    hit2 = lane == i2
    onehot = jnp.where(hit1 | hit2, 1.0, 0.0)
    rr = lax.broadcasted_iota(jnp.int32, (tr, tr), 0)
    cc = lax.broadcasted_iota(jnp.int32, (tr, tr), 1)
    lower = jnp.where(rr > cc, 1.0, 0.0).astype(BF16)
    before = jnp.dot(lower, onehot.astype(BF16), preferred_element_type=F32) + carry_ref[0:1]
    r1 = jnp.sum(jnp.where(hit1, before, 0.0), axis=-1, keepdims=True)
    r2 = jnp.sum(jnp.where(hit2, before, 0.0), axis=-1, keepdims=True)
    total = carry_ref[0:1] + jnp.sum(onehot, axis=0, keepdims=True)
    carry_ref[...] = jnp.broadcast_to(total, carry_ref.shape)
    cnt_ref[...] = jnp.broadcast_to(total, cnt_ref.shape)

    info = jnp.where(lane == 0, i1, 0.0)
    for idx, val in ((1, i2), (2, r1), (3, r2), (4, w1), (5, w2)):
        info = jnp.where(lane == idx, val, info)
    info_ref[...] = info


def router(x, gain, w_router, rows=256):
    s, d = x.shape
    n_exp = w_router.shape[1]
    tr = _tile(s, rows, 8)
    wr = jnp.zeros((d, LANES), F32).at[:, :n_exp].set(w_router.astype(F32))
    return pl.pallas_call(
        functools.partial(_router_body, n_exp=n_exp),
        grid=(s // tr,),
        in_specs=[pl.BlockSpec((tr, d), lambda i: (i, 0)),
                  pl.BlockSpec((1, d), lambda i: (0, 0)),
                  pl.BlockSpec((d, LANES), lambda i: (0, 0))],
        out_specs=[pl.BlockSpec((tr, LANES), lambda i: (i, 0)),
                   pl.BlockSpec((8, LANES), lambda i: (0, 0))],
        out_shape=[jax.ShapeDtypeStruct((s, LANES), F32),
                   jax.ShapeDtypeStruct((8, LANES), F32)],
        scratch_shapes=[pltpu.VMEM((8, LANES), F32)],
        compiler_params=_cparams(("arbitrary",)),
        name="moe_router",
    )(x, gain.reshape(1, d).astype(F32), wr)


def _dispatch_body(rows_ref, lo_ref, hi_ref, x_ref, g_ref, xg_ref, hbuf, zrow, sem, *, tt,
                   n_ranges):
    base = pl.program_id(0) * tt

    @pl.when(pl.program_id(0) == 0)
    def _():
        zrow[...] = jnp.zeros(zrow.shape, zrow.dtype)

        def zcopy(r):
            return pltpu.make_async_copy(zrow.at[pl.ds(0, 1), :], xg_ref.at[pl.ds(r, 1), :], sem)

        for e in range(n_ranges):
            lax.fori_loop(lo_ref[e], hi_ref[e], lambda r, c: (zcopy(r).start(), c)[1], 0)
        for e in range(n_ranges):
            lax.fori_loop(lo_ref[e], hi_ref[e], lambda r, c: (zcopy(r).wait(), c)[1], 0)

    hbuf[...] = _rms(x_ref[...], g_ref[...])

    def copy(n):
        return pltpu.make_async_copy(hbuf.at[pl.ds(n // TOP_K, 1), :],
                                     xg_ref.at[pl.ds(rows_ref[base * TOP_K + n], 1), :], sem)

    lax.fori_loop(0, tt * TOP_K, lambda n, c: (copy(n).start(), c)[1], 0)
    lax.fori_loop(0, tt * TOP_K, lambda n, c: (copy(n).wait(), c)[1], 0)


def dispatch(x, gain, rows, pad_lo, pad_hi, n_slots, tokens=256):
    s, d = x.shape
    tt = _tile(s, tokens, 8)
    grid_spec = pltpu.PrefetchScalarGridSpec(
        num_scalar_prefetch=3,
        grid=(s // tt,),
        in_specs=[pl.BlockSpec((tt, d), lambda i, *_: (i, 0)),
                  pl.BlockSpec((1, d), lambda i, *_: (0, 0))],
        out_specs=pl.BlockSpec(memory_space=pl.ANY),
        scratch_shapes=[pltpu.VMEM((tt, d), F32), pltpu.VMEM((8, d), F32),
                        pltpu.SemaphoreType.DMA(())],
    )
    return pl.pallas_call(
        functools.partial(_dispatch_body, tt=tt, n_ranges=pad_lo.shape[0]),
        grid_spec=grid_spec,
        out_shape=jax.ShapeDtypeStruct((n_slots, d), F32),
        compiler_params=_cparams(("arbitrary",)),
        name="moe_dispatch",
    )(rows, pad_lo, pad_hi, x, gain.reshape(1, d).astype(F32))


def _combine_body(rows_ref, x_ref, w_ref, y_ref, o_ref, buf, sem, *, tt):
    base = pl.program_id(0) * tt

    def copy(n):
        return pltpu.make_async_copy(y_ref.at[pl.ds(rows_ref[base * TOP_K + n], 1), :],
                                     buf.at[n % TOP_K, pl.ds(n // TOP_K, 1), :], sem)

    lax.fori_loop(0, tt * TOP_K, lambda n, c: (copy(n).start(), c)[1], 0)
    lax.fori_loop(0, tt * TOP_K, lambda n, c: (copy(n).wait(), c)[1], 0)
    w = w_ref[...]
    o_ref[...] = x_ref[...] + w[:, 4:5] * buf[0] + w[:, 5:6] * buf[1]


def combine(x, info, y, rows, tokens=256):
    s, d = x.shape
    tt = _tile(s, tokens, 8)
    grid_spec = pltpu.PrefetchScalarGridSpec(
        num_scalar_prefetch=1,
        grid=(s // tt,),
        in_specs=[pl.BlockSpec((tt, d), lambda i, r: (i, 0)),
                  pl.BlockSpec((tt, LANES), lambda i, r: (i, 0)),
                  pl.BlockSpec(memory_space=pl.ANY)],
        out_specs=pl.BlockSpec((tt, d), lambda i, r: (i, 0)),
        scratch_shapes=[pltpu.VMEM((TOP_K, tt, d), F32), pltpu.SemaphoreType.DMA(())],
    )
    return pl.pallas_call(
        functools.partial(_combine_body, tt=tt),
        grid_spec=grid_spec,
        out_shape=jax.ShapeDtypeStruct((s, d), F32),
        compiler_params=_cparams(("arbitrary",)),
        name="moe_combine",
    )(rows, x, info, y)


def _attention_layer(x, attn_norm, w_qkv, q_norm, k_norm, sinks, w_o):
    n_q_heads = w_o.shape[0] // HEAD_DIM
    h = rmsnorm(x, attn_norm)
    qkv = matmul(h, [w_qkv.astype(BF16)], name="qkv_proj")
    o = attention(qkv, q_norm, k_norm, sinks, n_q_heads)
    return matmul(o, [w_o.astype(BF16)], res=x, out_dtype=F32, name="attn_out_proj")


def _ffn_layer(x, ffn_norm, w_gate, w_up, w_down):
    h = rmsnorm(x, ffn_norm)
    act = matmul(h, [w_gate.astype(BF16), w_up.astype(BF16)], epilogue="swiglu", tn=512,
                 name="ffn_gate_up")
    return matmul(act, [w_down.astype(BF16)], res=x, out_dtype=F32, tk=3584, name="ffn_down")


def _s5_layer(x, ssm_norm, lam_re, lam_im, log_step, b_re, b_im, c_re, c_im, d_skip, w_glu):
    s, d = x.shape
    g = d // SSM_GROUP
    nc = s // SSM_CHUNK
    lc = SSM_CHUNK * SSM_GROUP
    h = rmsnorm(x, ssm_norm)
    u_t = h.reshape(nc, SSM_CHUNK, g, SSM_GROUP).transpose(2, 0, 1, 3).reshape(g, nc, lc)

    def two(a, b):
        return jnp.concatenate([a, b], axis=-1).astype(F32)
    prm = jnp.concatenate([
        two(b_re.transpose(0, 2, 1), b_im.transpose(0, 2, 1)),
        two(c_re, c_im),
        two(lam_re, lam_re)[:, None, :],
        two(lam_im, lam_im)[:, None, :],
        jnp.broadcast_to(log_step.astype(F32)[:, None, None], (g, 6, 2 * SSM_STATE)),
    ], axis=1)
    dsk = jnp.tile(d_skip.astype(F32).reshape(g, 1, SSM_GROUP), (1, 1, SSM_CHUNK))
    y_t = s5_mix(u_t, prm, dsk)
    y = y_t.reshape(g, nc, SSM_CHUNK, SSM_GROUP).transpose(1, 2, 0, 3).reshape(s, d)
    wg = w_glu.astype(BF16)
    return matmul(y, [wg, wg], epilogue="glu", res=x, out_dtype=F32, tn=512,
                  w_col_offsets=[0, d], n_out=d, name="s5_glu")


def _moe_layer(x, moe_norm, w_router, w_gate_e, w_up_e, w_down_e, tm=512):
    s, d = x.shape
    n_exp = w_router.shape[1]
    info, cnt = router(x, moe_norm, w_router)

    counts = cnt[0, :n_exp].astype(jnp.int32)
    padded = ((counts + tm - 1) // tm) * tm
    ends = jnp.cumsum(padded)
    starts = ends - padded
    n_slots = s * TOP_K + n_exp * tm
    n_tiles = n_slots // tm
    experts = info[:, 0:TOP_K].astype(jnp.int32)
    ranks = info[:, TOP_K:2 * TOP_K].astype(jnp.int32)
    rows = (starts[experts] + ranks).reshape(s * TOP_K)
    n_used = (ends[-1] // tm).astype(jnp.int32).reshape(1)
    tile_start = jnp.minimum(jnp.arange(n_tiles, dtype=jnp.int32), n_used[0] - 1) * tm
    tile_expert = jnp.sum(tile_start[:, None] >= ends[None, :], axis=1).astype(jnp.int32)

    pad_lo = jnp.concatenate([starts + counts, ends[-1:]]).astype(jnp.int32)
    pad_hi = jnp.concatenate([ends, jnp.full((1,), n_slots, ends.dtype)]).astype(jnp.int32)
    xg = dispatch(x, moe_norm, rows, pad_lo, pad_hi, n_slots)
    act = matmul(xg, [w_gate_e.astype(BF16), w_up_e.astype(BF16)], epilogue="swiglu",
                 tm=tm, tn=512, tile_expert=tile_expert, n_used=n_used, name="moe_gate_up")
    yg = matmul(act, [w_down_e.astype(BF16)], out_dtype=F32, tm=tm, tn=1024, tk=6144,
                tile_expert=tile_expert, n_used=n_used, name="moe_down")
    return combine(x, info, yg, rows)


def kernel(x, attn_norm, w_qkv, q_norm, k_norm, sinks, w_o, ffn_norm, w_gate, w_up, w_down,
           ssm_norm, lam_re, lam_im, log_step, b_re, b_im, c_re, c_im, d_skip, w_glu,
           moe_norm, w_router, w_gate_e, w_up_e, w_down_e):
    b, s, d = x.shape
    depth = attn_norm.shape[0] + ssm_norm.shape[0]
    outs = []
    for bi in range(b):
        xb = x[bi]
        for i in range(depth):
            j = i // 2
            if i % 2 == 0:
                xb = _attention_layer(xb, attn_norm[j], w_qkv[j], q_norm[j], k_norm[j],
                                      sinks[j], w_o[j])
                xb = _ffn_layer(xb, ffn_norm[j], w_gate[j], w_up[j], w_down[j])
            else:
                xb = _s5_layer(xb, ssm_norm[j], lam_re[j], lam_im[j], log_step[j], b_re[j],
                               b_im[j], c_re[j], c_im[j], d_skip[j], w_glu[j])
                xb = _moe_layer(xb, moe_norm[j], w_router[j], w_gate_e[j], w_up_e[j],
                                w_down_e[j])
        outs.append(xb)
    return jnp.stack(outs, axis=0)
```

```python
import functools

import jax
import jax.numpy as jnp
from jax import lax
from jax.experimental import pallas as pl
from jax.experimental.pallas import tpu as pltpu

F32 = jnp.float32
BF16 = jnp.bfloat16
HIGHEST = lax.Precision.HIGHEST

RMS_EPS = 1e-5
NEG_INF = -1e30

HEAD_DIM = 128
Q_PER_KV = 4
ATTN_BLOCK = 128
SSM_GROUP = 16
SSM_STATE = 64
SSM_CHUNK = 16
TOP_K = 2

LANES = 128
VMEM_LIMIT_MB = 56


def _cparams(semantics, vmem_mb=VMEM_LIMIT_MB):
    return pltpu.CompilerParams(dimension_semantics=semantics,
                                vmem_limit_bytes=vmem_mb << 20)


def _tile(n, pref, quantum=128):
    if n <= pref:
        return n
    t = (pref // quantum) * quantum
    while t > quantum and n % t:
        t -= quantum
    assert n % t == 0, (n, pref)
    return t


def _rms(x, gain):
    ms = jnp.mean(x * x, axis=-1, keepdims=True)
    return x * lax.rsqrt(ms + RMS_EPS) * gain


def _sigmoid(x):
    return 1.0 / (1.0 + jnp.exp(-x))


def _rmsnorm_body(x_ref, g_ref, o_ref):
    o_ref[...] = _rms(x_ref[...], g_ref[...]).astype(o_ref.dtype)


def rmsnorm(x, gain, rows=256):
    s, d = x.shape
    tr = _tile(s, rows, 8)
    return pl.pallas_call(
        _rmsnorm_body,
        grid=(s // tr,),
        in_specs=[pl.BlockSpec((tr, d), lambda i: (i, 0)),
                  pl.BlockSpec((1, d), lambda i: (0, 0))],
        out_specs=pl.BlockSpec((tr, d), lambda i: (i, 0)),
        out_shape=jax.ShapeDtypeStruct((s, d), BF16),
        compiler_params=_cparams(("parallel",)),
        name="rmsnorm",
    )(x, gain.reshape(1, d).astype(F32))


def _mm_body(*refs, n_rhs, nk, epilogue, has_res, grouped):
    pos = 0
    if grouped:
        nused_ref = refs[1]
        pos = 2
    a_ref = refs[pos]
    w_refs = refs[pos + 1:pos + 1 + n_rhs]
    pos += 1 + n_rhs
    res_ref = refs[pos] if has_res else None
    pos += int(has_res)
    o_ref = refs[pos]
    acc_refs = refs[pos + 1:]
    k = pl.program_id(2)

    def finish(parts):
        if epilogue == "swiglu":
            y = parts[0] * _sigmoid(parts[0]) * parts[1]
        elif epilogue == "glu":
            y = parts[0] * _sigmoid(parts[1])
        else:
            y = parts[0]
        if has_res:
            y = res_ref[...] + y
        o_ref[...] = y.astype(o_ref.dtype)

    def compute():
        a = a_ref[...].astype(BF16)
        parts = [jnp.dot(a, w[...], preferred_element_type=F32) for w in w_refs]
        if nk == 1:
            finish(parts)
            return

        @pl.when(k == 0)
        def _():
            for acc, p in zip(acc_refs, parts):
                acc[...] = p

        @pl.when(k > 0)
        def _():
            for acc, p in zip(acc_refs, parts):
                acc[...] += p

        @pl.when(k == nk - 1)
        def _():
            finish([acc[...] for acc in acc_refs])

    if grouped:
        valid = pl.program_id(0) < nused_ref[0]
        pl.when(valid)(compute)

        @pl.when(jnp.logical_not(valid) & (k == nk - 1))
        def _():
            o_ref[...] = jnp.zeros(o_ref.shape, o_ref.dtype)
    else:
        compute()


def matmul(a, ws, *, res=None, epilogue=None, out_dtype=BF16, tm=1024, tn=1024, tk=4096,
           w_col_offsets=None, n_out=None, tile_expert=None, n_used=None, name="matmul"):
    m, kdim = a.shape
    grouped = tile_expert is not None
    n_total = ws[0].shape[-1]
    n = n_total if n_out is None else n_out
    offs = [0] * len(ws) if w_col_offsets is None else w_col_offsets
    tm, tn, tk = _tile(m, tm), _tile(n, tn), _tile(kdim, tk)
    nk = kdim // tk
    grid = (m // tm, n // tn, nk)
    assert all(o % tn == 0 for o in offs)

    if grouped:
        def a_map(i, j, k, te, nu):
            v = i < nu[0]
            return (jnp.where(v, i, nu[0] - 1), jnp.where(v, k, 0))

        def w_map(off):
            def f(i, j, k, te, nu):
                v = i < nu[0]
                return (te[i], jnp.where(v, k, 0), jnp.where(v, j, 0) + off // tn)
            return f

        def o_map(i, j, k, te, nu):
            return (i, j)
        w_specs = [pl.BlockSpec((None, tk, tn), w_map(o)) for o in offs]
    else:
        def a_map(i, j, k):
            return (i, k)

        def w_map(off):
            return lambda i, j, k: (k, j + off // tn)

        def o_map(i, j, k):
            return (i, j)
        w_specs = [pl.BlockSpec((tk, tn), w_map(o)) for o in offs]

    in_specs = [pl.BlockSpec((tm, tk), a_map)] + w_specs
    args = [a] + list(ws)
    if res is not None:
        in_specs.append(pl.BlockSpec((tm, tn), o_map))
        args.append(res)
    scratch = [pltpu.VMEM((tm, tn), F32) for _ in ws] if nk > 1 else []
    body = functools.partial(_mm_body, n_rhs=len(ws), nk=nk, epilogue=epilogue,
                             has_res=res is not None, grouped=grouped)
    out_shape = jax.ShapeDtypeStruct((m, n), out_dtype)
    sem = ("parallel", "parallel", "arbitrary")
    if grouped:
        grid_spec = pltpu.PrefetchScalarGridSpec(
            num_scalar_prefetch=2, grid=grid, in_specs=in_specs,
            out_specs=pl.BlockSpec((tm, tn), o_map), scratch_shapes=scratch)
        return pl.pallas_call(body, grid_spec=grid_spec, out_shape=out_shape,
                              compiler_params=_cparams(sem), name=name)(
                                  tile_expert, n_used, *args)
    return pl.pallas_call(body, grid=grid, in_specs=in_specs,
                          out_specs=pl.BlockSpec((tm, tn), o_map), out_shape=out_shape,
                          scratch_shapes=scratch, compiler_params=_cparams(sem),
                          name=name)(*args)


def _attn_body(sink_ref, q_ref, kc_ref, kp_ref, vc_ref, vp_ref, qg_ref, kg_ref, o_ref, *, tq):
    i = pl.program_id(0)
    g = pl.program_id(1)
    blk = ATTN_BLOCK
    scale = HEAD_DIM ** -0.5
    qg = qg_ref[...]
    kg = kg_ref[...]
    k_all = jnp.concatenate([_rms(kp_ref[...].astype(F32), kg).astype(BF16),
                             _rms(kc_ref[...].astype(F32), kg).astype(BF16)], axis=0)
    v_all = jnp.concatenate([vp_ref[...], vc_ref[...]], axis=0)

    rows = Q_PER_KV * blk
    r = lax.broadcasted_iota(jnp.int32, (rows, 2 * blk), 0)
    c = lax.broadcasted_iota(jnp.int32, (rows, 2 * blk), 1)
    diff = (r % blk) + blk - c
    band = (diff >= 0) & (diff < blk)
    head = lax.broadcasted_iota(jnp.int32, (rows, 1), 0) // blk
    sink = jnp.zeros((rows, 1), F32)
    for a in range(Q_PER_KV):
        sink = jnp.where(head == a, sink_ref[g * Q_PER_KV + a], sink)

    for j in range(tq // blk):
        k2 = k_all[j * blk:(j + 2) * blk]
        v2 = v_all[j * blk:(j + 2) * blk]
        q = jnp.concatenate(
            [q_ref[j * blk:(j + 1) * blk, a * HEAD_DIM:(a + 1) * HEAD_DIM]
             for a in range(Q_PER_KV)], axis=0)
        qn = _rms(q.astype(F32), qg).astype(BF16)
        s = lax.dot_general(qn, k2, (((1,), (1,)), ((), ())),
                            preferred_element_type=F32) * scale
        if j == 0:
            valid = band & (c >= jnp.where(i > 0, 0, blk))
        else:
            valid = band
        s = jnp.where(valid, s, NEG_INF)
        m = jnp.maximum(jnp.max(s, axis=-1, keepdims=True), sink)
        e = jnp.exp(s - m)
        denom = jnp.sum(e, axis=-1, keepdims=True) + jnp.exp(sink - m)
        p = (e / denom).astype(BF16)
        o = jnp.dot(p, v2, preferred_element_type=F32)
        for a in range(Q_PER_KV):
            o_ref[j * blk:(j + 1) * blk, a * HEAD_DIM:(a + 1) * HEAD_DIM] = (
                o[a * blk:(a + 1) * blk].astype(o_ref.dtype))


def attention(qkv, q_gain, k_gain, sinks, n_q_heads, tq=512):
    s = qkv.shape[0]
    n_kv = n_q_heads // Q_PER_KV
    tq = _tile(s, tq, ATTN_BLOCK)
    sub = tq // ATTN_BLOCK
    qw = Q_PER_KV * HEAD_DIM
    k0 = n_q_heads
    v0 = n_q_heads + n_kv
    grid_spec = pltpu.PrefetchScalarGridSpec(
        num_scalar_prefetch=1,
        grid=(s // tq, n_kv),
        in_specs=[
            pl.BlockSpec((tq, qw), lambda i, g, sk: (i, g)),
            pl.BlockSpec((tq, HEAD_DIM), lambda i, g, sk: (i, k0 + g)),
            pl.BlockSpec((ATTN_BLOCK, HEAD_DIM),
                         lambda i, g, sk: (jnp.maximum(i * sub - 1, 0), k0 + g)),
            pl.BlockSpec((tq, HEAD_DIM), lambda i, g, sk: (i, v0 + g)),
            pl.BlockSpec((ATTN_BLOCK, HEAD_DIM),
                         lambda i, g, sk: (jnp.maximum(i * sub - 1, 0), v0 + g)),
            pl.BlockSpec((1, HEAD_DIM), lambda i, g, sk: (0, 0)),
            pl.BlockSpec((1, HEAD_DIM), lambda i, g, sk: (0, 0)),
        ],
        out_specs=pl.BlockSpec((tq, qw), lambda i, g, sk: (i, g)),
    )
    return pl.pallas_call(
        functools.partial(_attn_body, tq=tq),
        grid_spec=grid_spec,
        out_shape=jax.ShapeDtypeStruct((s, n_q_heads * HEAD_DIM), BF16),
        compiler_params=_cparams(("parallel", "parallel")),
        name="swa_attention",
    )(sinks.astype(F32), qkv, qkv, qkv, qkv, qkv,
      q_gain.reshape(1, HEAD_DIM).astype(F32), k_gain.reshape(1, HEAD_DIM).astype(F32))


SLAB_GROUPS = LANES // SSM_GROUP
CHUNK_PAIRS = SSM_CHUNK // 2
STATE_LANES = 2 * SSM_STATE


def _s5_group_tables(prm):
    p2 = STATE_LANES
    lc = SSM_CHUNK * SSM_GROUP
    bt = prm[0:16]
    cc = prm[16:32]
    lam_re = prm[32:33]
    lam_im = prm[33:34]
    step = jnp.exp(prm[34:35])
    lane = lax.broadcasted_iota(jnp.int32, (1, p2), 1)
    sgn = jnp.where(lane < SSM_STATE, -1.0, 1.0).astype(F32)

    def swap(x):
        return pltpu.roll(x, SSM_STATE, 1)

    def powers(expo):
        mag = jnp.exp(expo * (lam_re * step))
        ang = expo * (lam_im * step)
        return mag * jnp.cos(ang), mag * jnp.sin(ang)

    row = lax.broadcasted_iota(jnp.int32, (48, 1), 0)
    expo = jnp.where(row < 32, row, 16 * (1 << jnp.maximum(row - 32, 0))).astype(F32)
    pr, pi = powers(expo)

    lbr, lbi = pr[1:2], pi[1:2]
    inv = 1.0 / (lam_re * lam_re + lam_im * lam_im)
    cf_re = ((lbr - 1.0) * lam_re + lbi * lam_im) * inv
    cf_im = (lbi * lam_re - (lbr - 1.0) * lam_im) * inv
    bb1 = cf_re * bt + sgn * cf_im * swap(bt)
    bb2 = swap(bb1)
    ca = jnp.where(lane < SSM_STATE, cc, -cc)
    cb = -swap(cc)

    def rep_rows(x):
        return jnp.broadcast_to(x[:, None, :], (16, SSM_GROUP, p2)).reshape(lc, p2)

    def tile_rows(x):
        return jnp.broadcast_to(x[None, :, :], (SSM_CHUNK, 16, p2)).reshape(lc, p2)

    rrow = lax.broadcasted_iota(jnp.int32, (16, 1), 0)
    qr, qi = powers((SSM_CHUNK - 1 - rrow).astype(F32))
    s_mat = tile_rows(bb1) * rep_rows(qr) + tile_rows(bb2) * rep_rows(sgn * qi)
    c_t = tile_rows(ca) * rep_rows(pr[1:17]) + tile_rows(cb) * rep_rows(pi[1:17])
    f_t = tile_rows(ca) * rep_rows(pr[0:16]) + tile_rows(cb) * rep_rows(pi[0:16])
    kern = lax.dot_general(bb1, f_t, (((1,), (1,)), ((), ())),
                           precision=HIGHEST, preferred_element_type=F32)
    return s_mat, c_t, kern, pr, pi, sgn


def _s5_body(x_ref, prm_ref, dsk_ref, y_ref, xs, xp, wt, ws, wc):
    nc = x_ref.shape[0] // SSM_CHUNK
    gw = SSM_GROUP
    sl = STATE_LANES

    @pl.when(pl.program_id(0) == 0)
    def _():
        ws[...] = jnp.zeros(ws.shape, ws.dtype)
        wc[...] = jnp.zeros(wc.shape, wc.dtype)

    lane_w = lax.broadcasted_iota(jnp.int32, (gw, 2 * LANES), 1)
    scan_tables = []
    for g in range(SLAB_GROUPS):
        s_mat, c_t, kern, pr, pi, sgn = _s5_group_tables(prm_ref[g])
        scan_tables.append((pr, pi, sgn))
        for s in range(SSM_CHUNK):
            r0 = (s % 2) * LANES + g * gw
            ws[s // 2, r0:r0 + gw, g * sl:(g + 1) * sl] = s_mat[s * gw:(s + 1) * gw].astype(BF16)
            wc[s // 2, r0:r0 + gw, g * sl:(g + 1) * sl] = c_t[s * gw:(s + 1) * gw].astype(BF16)
        for d in range(CHUNK_PAIRS):
            for h in range(2):
                blk = jnp.zeros((gw, 2 * LANES), F32)
                for h2 in range(2):
                    tau = 2 * d + h2 - h
                    if 0 <= tau < SSM_CHUNK:
                        dst = h2 * LANES + g * gw
                        shift = (dst - tau * gw) % (2 * LANES)
                        piece = pltpu.roll(kern, shift, 1) if shift else kern
                        blk = jnp.where((lane_w >= dst) & (lane_w < dst + gw), piece, blk)
                r0 = h * LANES + g * gw
                wt[d, r0:r0 + gw, :] = blk.astype(BF16)

    xs[...] = x_ref[...].astype(F32)
    for sp in range(CHUNK_PAIRS):
        xp[sp] = jnp.concatenate(
            [xs[pl.ds(2 * sp, nc, stride=SSM_CHUNK), :],
             xs[pl.ds(2 * sp + 1, nc, stride=SSM_CHUNK), :]], axis=1).astype(BF16)

    v = jnp.dot(xp[0], ws[0], preferred_element_type=F32)
    for sp in range(1, CHUNK_PAIRS):
        v = v + jnp.dot(xp[sp], ws[sp], preferred_element_type=F32)

    rown = lax.broadcasted_iota(jnp.int32, (nc, 1), 0)
    h_in = []
    for g in range(SLAB_GROUPS):
        pr, pi, sgn = scan_tables[g]
        x = v[:, g * sl:(g + 1) * sl]
        d, j = 1, 0
        while d < nc:
            ar = pr[32 + j:33 + j]
            ai = sgn * pi[32 + j:33 + j]
            sh = jnp.where(rown >= d, pltpu.roll(x, d, 0), 0.0)
            x = x + ar * sh + ai * pltpu.roll(sh, SSM_STATE, 1)
            d, j = d * 2, j + 1
        h_in.append(jnp.where(rown >= 1, pltpu.roll(x, 1, 0), 0.0).astype(BF16))
    h_in = jnp.concatenate(h_in, axis=1)

    dsk = dsk_ref[...]
    for tp in range(CHUNK_PAIRS):
        acc = lax.dot_general(h_in, wc[tp], (((1,), (1,)), ((), ())),
                              preferred_element_type=F32)
        for sp in range(tp + 1):
            acc = acc + jnp.dot(xp[sp], wt[tp - sp], preferred_element_type=F32)
        for h2 in range(2):
            rows = pl.ds(2 * tp + h2, nc, stride=SSM_CHUNK)
            z = acc[:, h2 * LANES:(h2 + 1) * LANES] + dsk * xs[rows, :]
            xs[rows, :] = 0.5 * z * (1.0 + jnp.tanh(
                0.7978845608028654 * (z + 0.044715 * (z * z * z))))
    y_ref[...] = xs[...].astype(y_ref.dtype)


def s5_mix(h, prm, d_skip):
    s, d = h.shape
    nc = s // SSM_CHUNK
    return pl.pallas_call(
        _s5_body,
        grid=(d // LANES,),
        in_specs=[pl.BlockSpec((s, LANES), lambda j: (0, j)),
                  pl.BlockSpec((SLAB_GROUPS, 40, STATE_LANES), lambda j: (j, 0, 0)),
                  pl.BlockSpec((1, LANES), lambda j: (0, j))],
        out_specs=pl.BlockSpec((s, LANES), lambda j: (0, j)),
        out_shape=jax.ShapeDtypeStruct((s, d), BF16),
        scratch_shapes=[
            pltpu.VMEM((s, LANES), F32),
            pltpu.VMEM((CHUNK_PAIRS, nc, 2 * LANES), BF16),
            pltpu.VMEM((CHUNK_PAIRS, 2 * LANES, 2 * LANES), BF16),
            pltpu.VMEM((CHUNK_PAIRS, 2 * LANES, SLAB_GROUPS * STATE_LANES), BF16),
            pltpu.VMEM((CHUNK_PAIRS, 2 * LANES, SLAB_GROUPS * STATE_LANES), BF16),
        ],
        compiler_params=_cparams(("arbitrary",)),
        name="s5_mix",
    )(h, prm, d_skip.reshape(1, d).astype(F32))


def _router_body(x_ref, g_ref, wr_ref, info_ref, cnt_ref, carry_ref, *, n_exp):
    i = pl.program_id(0)
    tr = x_ref.shape[0]

    @pl.when(i == 0)
    def _():
        carry_ref[...] = jnp.zeros(carry_ref.shape, F32)

    h = _rms(x_ref[...], g_ref[...])
    logits = jnp.dot(h, wr_ref[...], precision=HIGHEST, preferred_element_type=F32)
    lane = lax.broadcasted_iota(jnp.int32, (tr, LANES), 1).astype(F32)
    neg = jnp.float32(-jnp.inf)
    l1 = jnp.where(lane < n_exp, logits, neg)
    m1 = jnp.max(l1, axis=-1, keepdims=True)
    i1 = jnp.min(jnp.where(l1 == m1, lane, float(LANES)), axis=-1, keepdims=True)
    l2 = jnp.where(lane == i1, neg, l1)
    m2 = jnp.max(l2, axis=-1, keepdims=True)
    i2 = jnp.min(jnp.where(l2 == m2, lane, float(LANES)), axis=-1, keepdims=True)
    e = jnp.exp(m2 - m1)
    w1 = 1.0 / (1.0 + e)
    w2 = e / (1.0 + e)

    hit1 = lane == i1
    hit2 = lane == i2
    onehot = jnp.where(hit1 | hit2, 1.0, 0.0)
    rr = lax.broadcasted_iota(jnp.int32, (tr, tr), 0)
    cc = lax.broadcasted_iota(jnp.int32, (tr, tr), 1)
    lower = jnp.where(rr > cc, 1.0, 0.0).astype(BF16)
    before = jnp.dot(lower, onehot.astype(BF16), preferred_element_type=F32) + carry_ref[0:1]
    r1 = jnp.sum(jnp.where(hit1, before, 0.0), axis=-1, keepdims=True)
    r2 = jnp.sum(jnp.where(hit2, before, 0.0), axis=-1, keepdims=True)
    total = carry_ref[0:1] + jnp.sum(onehot, axis=0, keepdims=True)
    carry_ref[...] = jnp.broadcast_to(total, carry_ref.shape)
    cnt_ref[...] = jnp.broadcast_to(total, cnt_ref.shape)

    info = jnp.where(lane == 0, i1, 0.0)
    for idx, val in ((1, i2), (2, r1), (3, r2), (4, w1), (5, w2)):
        info = jnp.where(lane == idx, val, info)
    info_ref[...] = info


def router(x, gain, w_router, rows=256):
    s, d = x.shape
    n_exp = w_router.shape[1]
    tr = _tile(s, rows, 8)
    wr = jnp.zeros((d, LANES), F32).at[:, :n_exp].set(w_router.astype(F32))
    return pl.pallas_call(
        functools.partial(_router_body, n_exp=n_exp),
        grid=(s // tr,),
        in_specs=[pl.BlockSpec((tr, d), lambda i: (i, 0)),
                  pl.BlockSpec((1, d), lambda i: (0, 0)),
                  pl.BlockSpec((d, LANES), lambda i: (0, 0))],
        out_specs=[pl.BlockSpec((tr, LANES), lambda i: (i, 0)),
                   pl.BlockSpec((8, LANES), lambda i: (0, 0))],
        out_shape=[jax.ShapeDtypeStruct((s, LANES), F32),
                   jax.ShapeDtypeStruct((8, LANES), F32)],
        scratch_shapes=[pltpu.VMEM((8, LANES), F32)],
        compiler_params=_cparams(("arbitrary",)),
        name="moe_router",
    )(x, gain.reshape(1, d).astype(F32), wr)


def _dispatch_body(rows_ref, lo_ref, hi_ref, x_ref, g_ref, xg_ref, hbuf, zrow, sem, *, tt,
                   n_ranges):
    base = pl.program_id(0) * tt

    @pl.when(pl.program_id(0) == 0)
    def _():
        zrow[...] = jnp.zeros(zrow.shape, zrow.dtype)

        def zcopy(r):
            return pltpu.make_async_copy(zrow.at[pl.ds(0, 1), :], xg_ref.at[pl.ds(r, 1), :], sem)

        for e in range(n_ranges):
            lax.fori_loop(lo_ref[e], hi_ref[e], lambda r, c: (zcopy(r).start(), c)[1], 0)
        for e in range(n_ranges):
            lax.fori_loop(lo_ref[e], hi_ref[e], lambda r, c: (zcopy(r).wait(), c)[1], 0)

    hbuf[...] = _rms(x_ref[...], g_ref[...])

    def copy(n):
        return pltpu.make_async_copy(hbuf.at[pl.ds(n // TOP_K, 1), :],
                                     xg_ref.at[pl.ds(rows_ref[base * TOP_K + n], 1), :], sem)

    lax.fori_loop(0, tt * TOP_K, lambda n, c: (copy(n).start(), c)[1], 0)
    lax.fori_loop(0, tt * TOP_K, lambda n, c: (copy(n).wait(), c)[1], 0)


def dispatch(x, gain, rows, pad_lo, pad_hi, n_slots, tokens=256):
    s, d = x.shape
    tt = _tile(s, tokens, 8)
    grid_spec = pltpu.PrefetchScalarGridSpec(
        num_scalar_prefetch=3,
        grid=(s // tt,),
        in_specs=[pl.BlockSpec((tt, d), lambda i, *_: (i, 0)),
                  pl.BlockSpec((1, d), lambda i, *_: (0, 0))],
        out_specs=pl.BlockSpec(memory_space=pl.ANY),
        scratch_shapes=[pltpu.VMEM((tt, d), F32), pltpu.VMEM((8, d), F32),
                        pltpu.SemaphoreType.DMA(())],
    )
    return pl.pallas_call(
        functools.partial(_dispatch_body, tt=tt, n_ranges=pad_lo.shape[0]),
        grid_spec=grid_spec,
        out_shape=jax.ShapeDtypeStruct((n_slots, d), F32),
        compiler_params=_cparams(("arbitrary",)),
        name="moe_dispatch",
    )(rows, pad_lo, pad_hi, x, gain.reshape(1, d).astype(F32))


def _combine_body(rows_ref, x_ref, w_ref, y_ref, o_ref, buf, sem, *, tt):
    base = pl.program_id(0) * tt

    def copy(n):
        return pltpu.make_async_copy(y_ref.at[pl.ds(rows_ref[base * TOP_K + n], 1), :],
                                     buf.at[n % TOP_K, pl.ds(n // TOP_K, 1), :], sem)

    lax.fori_loop(0, tt * TOP_K, lambda n, c: (copy(n).start(), c)[1], 0)
    lax.fori_loop(0, tt * TOP_K, lambda n, c: (copy(n).wait(), c)[1], 0)
    w = w_ref[...]
    o_ref[...] = x_ref[...] + w[:, 4:5] * buf[0] + w[:, 5:6] * buf[1]


def combine(x, info, y, rows, tokens=256):
    s, d = x.shape
    tt = _tile(s, tokens, 8)
    grid_spec = pltpu.PrefetchScalarGridSpec(
        num_scalar_prefetch=1,
        grid=(s // tt,),
        in_specs=[pl.BlockSpec((tt, d), lambda i, r: (i, 0)),
                  pl.BlockSpec((tt, LANES), lambda i, r: (i, 0)),
                  pl.BlockSpec(memory_space=pl.ANY)],
        out_specs=pl.BlockSpec((tt, d), lambda i, r: (i, 0)),
        scratch_shapes=[pltpu.VMEM((TOP_K, tt, d), F32), pltpu.SemaphoreType.DMA(())],
    )
    return pl.pallas_call(
        functools.partial(_combine_body, tt=tt),
        grid_spec=grid_spec,
        out_shape=jax.ShapeDtypeStruct((s, d), F32),
        compiler_params=_cparams(("arbitrary",)),
        name="moe_combine",
    )(rows, x, info, y)


def _attention_layer(x, attn_norm, w_qkv, q_norm, k_norm, sinks, w_o):
    n_q_heads = w_o.shape[0] // HEAD_DIM
    h = rmsnorm(x, attn_norm)
    qkv = matmul(h, [w_qkv.astype(BF16)], name="qkv_proj")
    o = attention(qkv, q_norm, k_norm, sinks, n_q_heads)
    return matmul(o, [w_o.astype(BF16)], res=x, out_dtype=F32, name="attn_out_proj")


def _ffn_layer(x, ffn_norm, w_gate, w_up, w_down):
    h = rmsnorm(x, ffn_norm)
    act = matmul(h, [w_gate.astype(BF16), w_up.astype(BF16)], epilogue="swiglu", tn=512,
                 name="ffn_gate_up")
    return matmul(act, [w_down.astype(BF16)], res=x, out_dtype=F32, tk=3584, name="ffn_down")


def _s5_layer(x, ssm_norm, lam_re, lam_im, log_step, b_re, b_im, c_re, c_im, d_skip, w_glu):
    s, d = x.shape
    g = d // SSM_GROUP
    h = rmsnorm(x, ssm_norm)

    def two(a, b):
        return jnp.concatenate([a, b], axis=-1).astype(F32)
    prm = jnp.concatenate([
        two(b_re.transpose(0, 2, 1), b_im.transpose(0, 2, 1)),
        two(c_re, c_im),
        two(lam_re, lam_re)[:, None, :],
        two(lam_im, lam_im)[:, None, :],
        jnp.broadcast_to(log_step.astype(F32)[:, None, None], (g, 6, 2 * SSM_STATE)),
    ], axis=1)
    y = s5_mix(h, prm, d_skip)
    wg = w_glu.astype(BF16)
    return matmul(y, [wg, wg], epilogue="glu", res=x, out_dtype=F32, tn=512,
                  w_col_offsets=[0, d], n_out=d, name="s5_glu")


def _moe_layer(x, moe_norm, w_router, w_gate_e, w_up_e, w_down_e, tm=512):
    s, d = x.shape
    n_exp = w_router.shape[1]
    info, cnt = router(x, moe_norm, w_router)

    counts = cnt[0, :n_exp].astype(jnp.int32)
    padded = ((counts + tm - 1) // tm) * tm
    ends = jnp.cumsum(padded)
    starts = ends - padded
    n_slots = s * TOP_K + n_exp * tm
    n_tiles = n_slots // tm
    experts = info[:, 0:TOP_K].astype(jnp.int32)
    ranks = info[:, TOP_K:2 * TOP_K].astype(jnp.int32)
    rows = (starts[experts] + ranks).reshape(s * TOP_K)
    n_used = (ends[-1] // tm).astype(jnp.int32).reshape(1)
    tile_start = jnp.minimum(jnp.arange(n_tiles, dtype=jnp.int32), n_used[0] - 1) * tm
    tile_expert = jnp.sum(tile_start[:, None] >= ends[None, :], axis=1).astype(jnp.int32)

    pad_lo = jnp.concatenate([starts + counts, ends[-1:]]).astype(jnp.int32)
    pad_hi = jnp.concatenate([ends, jnp.full((1,), n_slots, ends.dtype)]).astype(jnp.int32)
    xg = dispatch(x, moe_norm, rows, pad_lo, pad_hi, n_slots)
    act = matmul(xg, [w_gate_e.astype(BF16), w_up_e.astype(BF16)], epilogue="swiglu",
                 tm=tm, tn=512, tile_expert=tile_expert, n_used=n_used, name="moe_gate_up")
    yg = matmul(act, [w_down_e.astype(BF16)], out_dtype=F32, tm=tm, tn=1024, tk=6144,
                tile_expert=tile_expert, n_used=n_used, name="moe_down")
    return combine(x, info, yg, rows)


def kernel(x, attn_norm, w_qkv, q_norm, k_norm, sinks, w_o, ffn_norm, w_gate, w_up, w_down,
           ssm_norm, lam_re, lam_im, log_step, b_re, b_im, c_re, c_im, d_skip, w_glu,
           moe_norm, w_router, w_gate_e, w_up_e, w_down_e):
    b, s, d = x.shape
    depth = attn_norm.shape[0] + ssm_norm.shape[0]
    outs = []
    for bi in range(b):
        xb = x[bi]
        for i in range(depth):
            j = i // 2
            if i % 2 == 0:
                xb = _attention_layer(xb, attn_norm[j], w_qkv[j], q_norm[j], k_norm[j],
                                      sinks[j], w_o[j])
                xb = _ffn_layer(xb, ffn_norm[j], w_gate[j], w_up[j], w_down[j])
            else:
                xb = _s5_layer(xb, ssm_norm[j], lam_re[j], lam_im[j], log_step[j], b_re[j],
                               b_im[j], c_re[j], c_im[j], d_skip[j], w_glu[j])
                xb = _moe_layer(xb, moe_norm[j], w_router[j], w_gate_e[j], w_up_e[j],
                                w_down_e[j])
        outs.append(xb)
    return jnp.stack(outs, axis=0)
```

```python
import functools

import jax
import jax.numpy as jnp
from jax import lax
from jax.experimental import pallas as pl
from jax.experimental.pallas import tpu as pltpu

F32 = jnp.float32
BF16 = jnp.bfloat16
HIGHEST = lax.Precision.HIGHEST

RMS_EPS = 1e-5
NEG_INF = -1e30

HEAD_DIM = 128
Q_PER_KV = 4
ATTN_BLOCK = 128
SSM_GROUP = 16
SSM_STATE = 64
SSM_CHUNK = 16
TOP_K = 2

LANES = 128
VMEM_LIMIT_MB = 56


def _cparams(semantics, vmem_mb=VMEM_LIMIT_MB, row_dma_loop=False):
    return pltpu.CompilerParams(dimension_semantics=semantics,
                                vmem_limit_bytes=vmem_mb << 20,
                                disable_bounds_checks=row_dma_loop)


def _tile(n, pref, quantum=128):
    if n <= pref:
        return n
    t = (pref // quantum) * quantum
    while t > quantum and n % t:
        t -= quantum
    assert n % t == 0, (n, pref)
    return t


def _rms(x, gain):
    ms = jnp.mean(x * x, axis=-1, keepdims=True)
    return x * lax.rsqrt(ms + RMS_EPS) * gain


def _sigmoid(x):
    return 1.0 / (1.0 + jnp.exp(-x))


def _rmsnorm_body(x_ref, g_ref, o_ref):
    o_ref[...] = _rms(x_ref[...], g_ref[...]).astype(o_ref.dtype)


def rmsnorm(x, gain, rows=256):
    s, d = x.shape
    tr = _tile(s, rows, 8)
    return pl.pallas_call(
        _rmsnorm_body,
        grid=(s // tr,),
        in_specs=[pl.BlockSpec((tr, d), lambda i: (i, 0)),
                  pl.BlockSpec((1, d), lambda i: (0, 0))],
        out_specs=pl.BlockSpec((tr, d), lambda i: (i, 0)),
        out_shape=jax.ShapeDtypeStruct((s, d), BF16),
        compiler_params=_cparams(("parallel",)),
        name="rmsnorm",
    )(x, gain.reshape(1, d).astype(F32))


def _mm_body(*refs, n_rhs, nk, epilogue, has_res, grouped, n_side):
    pos = 0
    if grouped:
        nused_ref = refs[1]
        pos = 2
    a_ref = refs[pos]
    w_refs = refs[pos + 1:pos + 1 + n_rhs]
    pos += 1 + n_rhs
    res_ref = refs[pos] if has_res else None
    pos += int(has_res)
    side_in = refs[pos:pos + n_side]
    pos += n_side
    o_ref = refs[pos]
    side_out = refs[pos + 1:pos + 1 + n_side]
    acc_refs = refs[pos + 1 + n_side:]
    k = pl.program_id(2)

    for src, dst in zip(side_in, side_out):
        dst[...] = src[...].astype(dst.dtype)

    def finish(parts):
        if epilogue == "swiglu":
            y = parts[0] * _sigmoid(parts[0]) * parts[1]
        elif epilogue == "glu":
            y = parts[0] * _sigmoid(parts[1])
        else:
            y = parts[0]
        if has_res:
            y = res_ref[...] + y
        o_ref[...] = y.astype(o_ref.dtype)

    def compute():
        a = a_ref[...].astype(BF16)
        parts = [jnp.dot(a, w[...], preferred_element_type=F32) for w in w_refs]
        if nk == 1:
            finish(parts)
            return

        @pl.when(k == 0)
        def _():
            for acc, p in zip(acc_refs, parts):
                acc[...] = p

        @pl.when(k > 0)
        def _():
            for acc, p in zip(acc_refs, parts):
                acc[...] += p

        @pl.when(k == nk - 1)
        def _():
            finish([acc[...] for acc in acc_refs])

    if grouped:
        valid = pl.program_id(0) < nused_ref[0]
        pl.when(valid)(compute)

        @pl.when(jnp.logical_not(valid) & (k == nk - 1))
        def _():
            o_ref[...] = jnp.zeros(o_ref.shape, o_ref.dtype)
    else:
        compute()


def matmul(a, ws, *, res=None, epilogue=None, out_dtype=BF16, tm=1024, tn=1024, tk=4096,
           w_col_offsets=None, n_out=None, tile_expert=None, n_used=None, side=(),
           name="matmul"):
    m, kdim = a.shape
    grouped = tile_expert is not None
    n_total = ws[0].shape[-1]
    n = n_total if n_out is None else n_out
    offs = [0] * len(ws) if w_col_offsets is None else w_col_offsets
    tm, tn, tk = _tile(m, tm), _tile(n, tn), _tile(kdim, tk)
    nk = kdim // tk
    grid = (m // tm, n // tn, nk)
    assert all(o % tn == 0 for o in offs)

    if grouped:
        def a_map(i, j, k, te, nu):
            v = i < nu[0]
            return (jnp.where(v, i, nu[0] - 1), jnp.where(v, k, 0))

        def w_map(off):
            def f(i, j, k, te, nu):
                v = i < nu[0]
                return (te[i], jnp.where(v, k, 0), jnp.where(v, j, 0) + off // tn)
            return f

        def o_map(i, j, k, te, nu):
            return (i, j)
        w_specs = [pl.BlockSpec((None, tk, tn), w_map(o)) for o in offs]
    else:
        def a_map(i, j, k):
            return (i, k)

        def w_map(off):
            return lambda i, j, k: (k, j + off // tn)

        def o_map(i, j, k):
            return (i, j)
        w_specs = [pl.BlockSpec((tk, tn), w_map(o)) for o in offs]

    in_specs = [pl.BlockSpec((tm, tk), a_map)] + w_specs
    args = [a] + list(ws)
    if res is not None:
        in_specs.append(pl.BlockSpec((tm, tn), o_map))
        args.append(res)
    scratch = [pltpu.VMEM((tm, tn), F32) for _ in ws] if nk > 1 else []
    body = functools.partial(_mm_body, n_rhs=len(ws), nk=nk, epilogue=epilogue,
                             has_res=res is not None, grouped=grouped, n_side=len(side))
    out_shape = jax.ShapeDtypeStruct((m, n), out_dtype)
    sem = ("parallel", "parallel", "arbitrary")
    if side:
        assert not grouped
        steps = grid[0] * grid[1] * grid[2]
        out_specs = [pl.BlockSpec((tm, tn), o_map)]
        out_shape = [out_shape]
        for arr in side:
            rows, cols = arr.shape
            nblk = max(c for c in range(1, steps + 1)
                       if rows % c == 0 and (rows // c) % 16 == 0)

            def s_map(i, j, k, nblk=nblk):
                return (jnp.minimum((i * grid[1] + j) * grid[2] + k, nblk - 1), 0)
            in_specs.append(pl.BlockSpec((rows // nblk, cols), s_map))
            out_specs.append(pl.BlockSpec((rows // nblk, cols), s_map))
            out_shape.append(jax.ShapeDtypeStruct(arr.shape, BF16))
            args.append(arr)
        outs = pl.pallas_call(body, grid=grid, in_specs=in_specs, out_specs=out_specs,
                              out_shape=out_shape, scratch_shapes=scratch,
                              compiler_params=_cparams(("arbitrary",) * 3), name=name)(*args)
        return outs[0], list(outs[1:])
    if grouped:
        grid_spec = pltpu.PrefetchScalarGridSpec(
            num_scalar_prefetch=2, grid=grid, in_specs=in_specs,
            out_specs=pl.BlockSpec((tm, tn), o_map), scratch_shapes=scratch)
        return pl.pallas_call(body, grid_spec=grid_spec, out_shape=out_shape,
                              compiler_params=_cparams(sem), name=name)(
                                  tile_expert, n_used, *args)
    return pl.pallas_call(body, grid=grid, in_specs=in_specs,
                          out_specs=pl.BlockSpec((tm, tn), o_map), out_shape=out_shape,
                          scratch_shapes=scratch, compiler_params=_cparams(sem),
                          name=name)(*args)


def _attn_body(sink_ref, q_ref, kc_ref, kp_ref, vc_ref, vp_ref, qg_ref, kg_ref, o_ref, *, tq):
    i = pl.program_id(0)
    g = pl.program_id(1)
    blk = ATTN_BLOCK
    scale = HEAD_DIM ** -0.5
    qg = qg_ref[...]
    kg = kg_ref[...]
    k_all = jnp.concatenate([_rms(kp_ref[...].astype(F32), kg).astype(BF16),
                             _rms(kc_ref[...].astype(F32), kg).astype(BF16)], axis=0)
    v_all = jnp.concatenate([vp_ref[...], vc_ref[...]], axis=0)

    rows = Q_PER_KV * blk
    r = lax.broadcasted_iota(jnp.int32, (rows, 2 * blk), 0)
    c = lax.broadcasted_iota(jnp.int32, (rows, 2 * blk), 1)
    diff = (r % blk) + blk - c
    band = (diff >= 0) & (diff < blk)
    head = lax.broadcasted_iota(jnp.int32, (rows, 1), 0) // blk
    sink = jnp.zeros((rows, 1), F32)
    for a in range(Q_PER_KV):
        sink = jnp.where(head == a, sink_ref[g * Q_PER_KV + a], sink)

    for j in range(tq // blk):
        k2 = k_all[j * blk:(j + 2) * blk]
        v2 = v_all[j * blk:(j + 2) * blk]
        q = jnp.concatenate(
            [q_ref[j * blk:(j + 1) * blk, a * HEAD_DIM:(a + 1) * HEAD_DIM]
             for a in range(Q_PER_KV)], axis=0)
        qn = _rms(q.astype(F32), qg).astype(BF16)
        s = lax.dot_general(qn, k2, (((1,), (1,)), ((), ())),
                            preferred_element_type=F32) * scale
        if j == 0:
            valid = band & (c >= jnp.where(i > 0, 0, blk))
        else:
            valid = band
        s = jnp.where(valid, s, NEG_INF)
        m = jnp.maximum(jnp.max(s, axis=-1, keepdims=True), sink)
        e = jnp.exp(s - m)
        denom = jnp.sum(e, axis=-1, keepdims=True) + jnp.exp(sink - m)
        p = (e / denom).astype(BF16)
        o = jnp.dot(p, v2, preferred_element_type=F32)
        for a in range(Q_PER_KV):
            o_ref[j * blk:(j + 1) * blk, a * HEAD_DIM:(a + 1) * HEAD_DIM] = (
                o[a * blk:(a + 1) * blk].astype(o_ref.dtype))


def attention(qkv, q_gain, k_gain, sinks, n_q_heads, tq=512):
    s = qkv.shape[0]
    n_kv = n_q_heads // Q_PER_KV
    tq = _tile(s, tq, ATTN_BLOCK)
    sub = tq // ATTN_BLOCK
    qw = Q_PER_KV * HEAD_DIM
    k0 = n_q_heads
    v0 = n_q_heads + n_kv
    grid_spec = pltpu.PrefetchScalarGridSpec(
        num_scalar_prefetch=1,
        grid=(s // tq, n_kv),
        in_specs=[
            pl.BlockSpec((tq, qw), lambda i, g, sk: (i, g)),
            pl.BlockSpec((tq, HEAD_DIM), lambda i, g, sk: (i, k0 + g)),
            pl.BlockSpec((ATTN_BLOCK, HEAD_DIM),
                         lambda i, g, sk: (jnp.maximum(i * sub - 1, 0), k0 + g)),
            pl.BlockSpec((tq, HEAD_DIM), lambda i, g, sk: (i, v0 + g)),
            pl.BlockSpec((ATTN_BLOCK, HEAD_DIM),
                         lambda i, g, sk: (jnp.maximum(i * sub - 1, 0), v0 + g)),
            pl.BlockSpec((1, HEAD_DIM), lambda i, g, sk: (0, 0)),
            pl.BlockSpec((1, HEAD_DIM), lambda i, g, sk: (0, 0)),
        ],
        out_specs=pl.BlockSpec((tq, qw), lambda i, g, sk: (i, g)),
    )
    return pl.pallas_call(
        functools.partial(_attn_body, tq=tq),
        grid_spec=grid_spec,
        out_shape=jax.ShapeDtypeStruct((s, n_q_heads * HEAD_DIM), BF16),
        compiler_params=_cparams(("parallel", "parallel")),
        name="swa_attention",
    )(sinks.astype(F32), qkv, qkv, qkv, qkv, qkv,
      q_gain.reshape(1, HEAD_DIM).astype(F32), k_gain.reshape(1, HEAD_DIM).astype(F32))


SLAB_GROUPS = LANES // SSM_GROUP
CHUNK_PAIRS = SSM_CHUNK // 2
STATE_LANES = 2 * SSM_STATE


def _s5_group_tables(prm):
    p2 = STATE_LANES
    lc = SSM_CHUNK * SSM_GROUP
    bt = prm[0:16]
    cc = prm[16:32]
    lam_re = prm[32:33]
    lam_im = prm[33:34]
    step = jnp.exp(prm[34:35])
    lane = lax.broadcasted_iota(jnp.int32, (1, p2), 1)
    sgn = jnp.where(lane < SSM_STATE, -1.0, 1.0).astype(F32)

    def swap(x):
        return pltpu.roll(x, SSM_STATE, 1)

    def powers(expo):
        mag = jnp.exp(expo * (lam_re * step))
        ang = expo * (lam_im * step)
        return mag * jnp.cos(ang), mag * jnp.sin(ang)

    row = lax.broadcasted_iota(jnp.int32, (48, 1), 0)
    expo = jnp.where(row < 32, row, 16 * (1 << jnp.maximum(row - 32, 0))).astype(F32)
    pr, pi = powers(expo)

    lbr, lbi = pr[1:2], pi[1:2]
    inv = 1.0 / (lam_re * lam_re + lam_im * lam_im)
    cf_re = ((lbr - 1.0) * lam_re + lbi * lam_im) * inv
    cf_im = (lbi * lam_re - (lbr - 1.0) * lam_im) * inv
    bb1 = cf_re * bt + sgn * cf_im * swap(bt)
    bb2 = swap(bb1)
    ca = jnp.where(lane < SSM_STATE, cc, -cc)
    cb = -swap(cc)

    def rep_rows(x):
        return jnp.broadcast_to(x[:, None, :], (16, SSM_GROUP, p2)).reshape(lc, p2)

    def tile_rows(x):
        return jnp.broadcast_to(x[None, :, :], (SSM_CHUNK, 16, p2)).reshape(lc, p2)

    rrow = lax.broadcasted_iota(jnp.int32, (16, 1), 0)
    qr, qi = powers((SSM_CHUNK - 1 - rrow).astype(F32))
    s_mat = tile_rows(bb1) * rep_rows(qr) + tile_rows(bb2) * rep_rows(sgn * qi)
    c_t = tile_rows(ca) * rep_rows(pr[1:17]) + tile_rows(cb) * rep_rows(pi[1:17])
    f_t = tile_rows(ca) * rep_rows(pr[0:16]) + tile_rows(cb) * rep_rows(pi[0:16])
    kern = lax.dot_general(bb1, f_t, (((1,), (1,)), ((), ())),
                           precision=HIGHEST, preferred_element_type=F32)
    return s_mat, c_t, kern, pr, pi, sgn


def _s5_body(x_ref, prm_ref, dsk_ref, y_ref, xs, xp, wt, ws, wc):
    nc = x_ref.shape[0] // SSM_CHUNK
    gw = SSM_GROUP
    sl = STATE_LANES

    @pl.when(pl.program_id(0) == 0)
    def _():
        ws[...] = jnp.zeros(ws.shape, ws.dtype)
        wc[...] = jnp.zeros(wc.shape, wc.dtype)

    lane_w = lax.broadcasted_iota(jnp.int32, (gw, 2 * LANES), 1)
    scan_tables = []
    for g in range(SLAB_GROUPS):
        s_mat, c_t, kern, pr, pi, sgn = _s5_group_tables(prm_ref[g])
        scan_tables.append((pr, pi, sgn))
        for s in range(SSM_CHUNK):
            r0 = (s % 2) * LANES + g * gw
            ws[s // 2, r0:r0 + gw, g * sl:(g + 1) * sl] = s_mat[s * gw:(s + 1) * gw].astype(BF16)
            wc[s // 2, r0:r0 + gw, g * sl:(g + 1) * sl] = c_t[s * gw:(s + 1) * gw].astype(BF16)
        for d in range(CHUNK_PAIRS):
            for h in range(2):
                blk = jnp.zeros((gw, 2 * LANES), F32)
                for h2 in range(2):
                    tau = 2 * d + h2 - h
                    if 0 <= tau < SSM_CHUNK:
                        dst = h2 * LANES + g * gw
                        shift = (dst - tau * gw) % (2 * LANES)
                        piece = pltpu.roll(kern, shift, 1) if shift else kern
                        blk = jnp.where((lane_w >= dst) & (lane_w < dst + gw), piece, blk)
                r0 = h * LANES + g * gw
                wt[d, r0:r0 + gw, :] = blk.astype(BF16)

    xs[...] = x_ref[...].astype(F32)
    for sp in range(CHUNK_PAIRS):
        xp[sp] = jnp.concatenate(
            [xs[pl.ds(2 * sp, nc, stride=SSM_CHUNK), :],
             xs[pl.ds(2 * sp + 1, nc, stride=SSM_CHUNK), :]], axis=1).astype(BF16)

    v = jnp.dot(xp[0], ws[0], preferred_element_type=F32)
    for sp in range(1, CHUNK_PAIRS):
        v = v + jnp.dot(xp[sp], ws[sp], preferred_element_type=F32)

    rown = lax.broadcasted_iota(jnp.int32, (nc, 1), 0)
    h_in = []
    for g in range(SLAB_GROUPS):
        pr, pi, sgn = scan_tables[g]
        x = v[:, g * sl:(g + 1) * sl]
        d, j = 1, 0
        while d < nc:
            ar = pr[32 + j:33 + j]
            ai = sgn * pi[32 + j:33 + j]
            sh = jnp.where(rown >= d, pltpu.roll(x, d, 0), 0.0)
            x = x + ar * sh + ai * pltpu.roll(sh, SSM_STATE, 1)
            d, j = d * 2, j + 1
        h_in.append(jnp.where(rown >= 1, pltpu.roll(x, 1, 0), 0.0).astype(BF16))
    h_in = jnp.concatenate(h_in, axis=1)

    dsk = dsk_ref[...]
    for tp in range(CHUNK_PAIRS):
        acc = lax.dot_general(h_in, wc[tp], (((1,), (1,)), ((), ())),
                              preferred_element_type=F32)
        for sp in range(tp + 1):
            acc = acc + jnp.dot(xp[sp], wt[tp - sp], preferred_element_type=F32)
        for h2 in range(2):
            rows = pl.ds(2 * tp + h2, nc, stride=SSM_CHUNK)
            z = acc[:, h2 * LANES:(h2 + 1) * LANES] + dsk * xs[rows, :]
            xs[rows, :] = 0.5 * z * (1.0 + jnp.tanh(
                0.7978845608028654 * (z + 0.044715 * (z * z * z))))
    y_ref[...] = xs[...].astype(y_ref.dtype)


def s5_mix(h, prm, d_skip):
    s, d = h.shape
    nc = s // SSM_CHUNK
    return pl.pallas_call(
        _s5_body,
        grid=(d // LANES,),
        in_specs=[pl.BlockSpec((s, LANES), lambda j: (0, j)),
                  pl.BlockSpec((SLAB_GROUPS, 40, STATE_LANES), lambda j: (j, 0, 0)),
                  pl.BlockSpec((1, LANES), lambda j: (0, j))],
        out_specs=pl.BlockSpec((s, LANES), lambda j: (0, j)),
        out_shape=jax.ShapeDtypeStruct((s, d), BF16),
        scratch_shapes=[
            pltpu.VMEM((s, LANES), F32),
            pltpu.VMEM((CHUNK_PAIRS, nc, 2 * LANES), BF16),
            pltpu.VMEM((CHUNK_PAIRS, 2 * LANES, 2 * LANES), BF16),
            pltpu.VMEM((CHUNK_PAIRS, 2 * LANES, SLAB_GROUPS * STATE_LANES), BF16),
            pltpu.VMEM((CHUNK_PAIRS, 2 * LANES, SLAB_GROUPS * STATE_LANES), BF16),
        ],
        compiler_params=_cparams(("arbitrary",)),
        name="s5_mix",
    )(h, prm, d_skip.reshape(1, d).astype(F32))


def _router_body(x_ref, g_ref, wr_ref, info_ref, cnt_ref, carry_ref, *, n_exp):
    i = pl.program_id(0)
    tr = x_ref.shape[0]

    @pl.when(i == 0)
    def _():
        carry_ref[...] = jnp.zeros(carry_ref.shape, F32)

    h = _rms(x_ref[...], g_ref[...])
    logits = jnp.dot(h, wr_ref[...], precision=HIGHEST, preferred_element_type=F32)
    lane = lax.broadcasted_iota(jnp.int32, (tr, LANES), 1).astype(F32)
    neg = jnp.float32(-jnp.inf)
    l1 = jnp.where(lane < n_exp, logits, neg)
    m1 = jnp.max(l1, axis=-1, keepdims=True)
    i1 = jnp.min(jnp.where(l1 == m1, lane, float(LANES)), axis=-1, keepdims=True)
    l2 = jnp.where(lane == i1, neg, l1)
    m2 = jnp.max(l2, axis=-1, keepdims=True)
    i2 = jnp.min(jnp.where(l2 == m2, lane, float(LANES)), axis=-1, keepdims=True)
    e = jnp.exp(m2 - m1)
    w1 = 1.0 / (1.0 + e)
    w2 = e / (1.0 + e)

    hit1 = lane == i1
    hit2 = lane == i2
    onehot = jnp.where(hit1 | hit2, 1.0, 0.0)
    rr = lax.broadcasted_iota(jnp.int32, (tr, tr), 0)
    cc = lax.broadcasted_iota(jnp.int32, (tr, tr), 1)
    lower = jnp.where(rr > cc, 1.0, 0.0).astype(BF16)
    before = jnp.dot(lower, onehot.astype(BF16), preferred_element_type=F32) + carry_ref[0:1]
    r1 = jnp.sum(jnp.where(hit1, before, 0.0), axis=-1, keepdims=True)
    r2 = jnp.sum(jnp.where(hit2, before, 0.0), axis=-1, keepdims=True)
    total = carry_ref[0:1] + jnp.sum(onehot, axis=0, keepdims=True)
    carry_ref[...] = jnp.broadcast_to(total, carry_ref.shape)
    cnt_ref[...] = jnp.broadcast_to(total, cnt_ref.shape)

    info = jnp.where(lane == 0, i1, 0.0)
    for idx, val in ((1, i2), (2, r1), (3, r2), (4, w1), (5, w2)):
        info = jnp.where(lane == idx, val, info)
    info_ref[...] = info


def router(x, gain, w_router, rows=256):
    s, d = x.shape
    n_exp = w_router.shape[1]
    tr = _tile(s, rows, 8)
    wr = jnp.zeros((d, LANES), F32).at[:, :n_exp].set(w_router.astype(F32))
    return pl.pallas_call(
        functools.partial(_router_body, n_exp=n_exp),
        grid=(s // tr,),
        in_specs=[pl.BlockSpec((tr, d), lambda i: (i, 0)),
                  pl.BlockSpec((1, d), lambda i: (0, 0)),
                  pl.BlockSpec((d, LANES), lambda i: (0, 0))],
        out_specs=[pl.BlockSpec((tr, LANES), lambda i: (i, 0)),
                   pl.BlockSpec((8, LANES), lambda i: (0, 0))],
        out_shape=[jax.ShapeDtypeStruct((s, LANES), F32),
                   jax.ShapeDtypeStruct((8, LANES), F32)],
        scratch_shapes=[pltpu.VMEM((8, LANES), F32)],
        compiler_params=_cparams(("arbitrary",)),
        name="moe_router",
    )(x, gain.reshape(1, d).astype(F32), wr)


def _dispatch_body(rows_ref, lo_ref, hi_ref, x_ref, g_ref, xg_ref, hbuf, zrow, sem, *, tt,
                   n_ranges):
    base = pl.program_id(0) * tt

    @pl.when(pl.program_id(0) == 0)
    def _():
        zrow[...] = jnp.zeros(zrow.shape, zrow.dtype)

        def zcopy(r):
            return pltpu.make_async_copy(zrow.at[pl.ds(0, 1), :], xg_ref.at[pl.ds(r, 1), :], sem)

        for e in range(n_ranges):
            lax.fori_loop(lo_ref[e], hi_ref[e], lambda r, c: (zcopy(r).start(), c)[1], 0)
        for e in range(n_ranges):
            lax.fori_loop(lo_ref[e], hi_ref[e], lambda r, c: (zcopy(r).wait(), c)[1], 0)

    hbuf[...] = _rms(x_ref[...], g_ref[...])

    def copy(n):
        return pltpu.make_async_copy(hbuf.at[pl.ds(n // TOP_K, 1), :],
                                     xg_ref.at[pl.ds(rows_ref[base * TOP_K + n], 1), :], sem)

    lax.fori_loop(0, tt * TOP_K, lambda n, c: (copy(n).start(), c)[1], 0)
    lax.fori_loop(0, tt * TOP_K, lambda n, c: (copy(n).wait(), c)[1], 0)


def dispatch(x, gain, rows, pad_lo, pad_hi, n_slots, tokens=256):
    s, d = x.shape
    tt = _tile(s, tokens, 8)
    grid_spec = pltpu.PrefetchScalarGridSpec(
        num_scalar_prefetch=3,
        grid=(s // tt,),
        in_specs=[pl.BlockSpec((tt, d), lambda i, *_: (i, 0)),
                  pl.BlockSpec((1, d), lambda i, *_: (0, 0))],
        out_specs=pl.BlockSpec(memory_space=pl.ANY),
        scratch_shapes=[pltpu.VMEM((tt, d), F32), pltpu.VMEM((8, d), F32),
                        pltpu.SemaphoreType.DMA(())],
    )
    return pl.pallas_call(
        functools.partial(_dispatch_body, tt=tt, n_ranges=pad_lo.shape[0]),
        grid_spec=grid_spec,
        out_shape=jax.ShapeDtypeStruct((n_slots, d), F32),
        compiler_params=_cparams(("arbitrary",), row_dma_loop=True),
        name="moe_dispatch",
    )(rows, pad_lo, pad_hi, x, gain.reshape(1, d).astype(F32))


def _combine_body(rows_ref, x_ref, w_ref, y_ref, o_ref, buf, sem, *, tt):
    base = pl.program_id(0) * tt

    def copy(n):
        return pltpu.make_async_copy(y_ref.at[pl.ds(rows_ref[base * TOP_K + n], 1), :],
                                     buf.at[n % TOP_K, pl.ds(n // TOP_K, 1), :], sem)

    lax.fori_loop(0, tt * TOP_K, lambda n, c: (copy(n).start(), c)[1], 0)
    lax.fori_loop(0, tt * TOP_K, lambda n, c: (copy(n).wait(), c)[1], 0)
    w = w_ref[...]
    o_ref[...] = x_ref[...] + w[:, 4:5] * buf[0] + w[:, 5:6] * buf[1]


def combine(x, info, y, rows, tokens=256):
    s, d = x.shape
    tt = _tile(s, tokens, 8)
    grid_spec = pltpu.PrefetchScalarGridSpec(
        num_scalar_prefetch=1,
        grid=(s // tt,),
        in_specs=[pl.BlockSpec((tt, d), lambda i, r: (i, 0)),
                  pl.BlockSpec((tt, LANES), lambda i, r: (i, 0)),
                  pl.BlockSpec(memory_space=pl.ANY)],
        out_specs=pl.BlockSpec((tt, d), lambda i, r: (i, 0)),
        scratch_shapes=[pltpu.VMEM((TOP_K, tt, d), F32), pltpu.SemaphoreType.DMA(())],
    )
    return pl.pallas_call(
        functools.partial(_combine_body, tt=tt),
        grid_spec=grid_spec,
        out_shape=jax.ShapeDtypeStruct((s, d), F32),
        compiler_params=_cparams(("arbitrary",), row_dma_loop=True),
        name="moe_combine",
    )(rows, x, info, y)


def _attention_layer(x, attn_norm, w_qkv, q_norm, k_norm, sinks, w_o):
    n_q_heads = w_o.shape[0] // HEAD_DIM
    h = rmsnorm(x, attn_norm)
    qkv = matmul(h, [w_qkv.astype(BF16)], name="qkv_proj")
    o = attention(qkv, q_norm, k_norm, sinks, n_q_heads)
    return matmul(o, [w_o.astype(BF16)], res=x, out_dtype=F32, name="attn_out_proj")


def _ffn_layer(x, ffn_norm, w_gate, w_up, w_down, later_weights=()):
    h = rmsnorm(x, ffn_norm)
    side = [w.reshape(-1, w.shape[-1]) for w in later_weights]
    res = matmul(h, [w_gate.astype(BF16), w_up.astype(BF16)], epilogue="swiglu",
                 tn=256, side=side, name="ffn_gate_up")
    act, converted = res if side else (res, [])
    converted = [c.reshape(w.shape) for c, w in zip(converted, later_weights)]
    out = matmul(act, [w_down.astype(BF16)], res=x, out_dtype=F32, tk=3584, name="ffn_down")
    return out, converted


def _s5_layer(x, ssm_norm, lam_re, lam_im, log_step, b_re, b_im, c_re, c_im, d_skip, w_glu):
    s, d = x.shape
    g = d // SSM_GROUP
    h = rmsnorm(x, ssm_norm)

    def two(a, b):
        return jnp.concatenate([a, b], axis=-1).astype(F32)
    prm = jnp.concatenate([
        two(b_re.transpose(0, 2, 1), b_im.transpose(0, 2, 1)),
        two(c_re, c_im),
        two(lam_re, lam_re)[:, None, :],
        two(lam_im, lam_im)[:, None, :],
        jnp.broadcast_to(log_step.astype(F32)[:, None, None], (g, 6, 2 * SSM_STATE)),
    ], axis=1)
    y = s5_mix(h, prm, d_skip)
    wg = w_glu.astype(BF16)
    return matmul(y, [wg, wg], epilogue="glu", res=x, out_dtype=F32, tn=512,
                  w_col_offsets=[0, d], n_out=d, name="s5_glu")


def _moe_layer(x, moe_norm, w_router, w_gate_e, w_up_e, w_down_e, tm=512):
    s, d = x.shape
    n_exp = w_router.shape[1]
    info, cnt = router(x, moe_norm, w_router)

    counts = cnt[0, :n_exp].astype(jnp.int32)
    padded = ((counts + tm - 1) // tm) * tm
    ends = jnp.cumsum(padded)
    starts = ends - padded
    n_slots = s * TOP_K + n_exp * tm
    n_tiles = n_slots // tm
    experts = info[:, 0:TOP_K].astype(jnp.int32)
    ranks = info[:, TOP_K:2 * TOP_K].astype(jnp.int32)
    rows = (starts[experts] + ranks).reshape(s * TOP_K)
    n_used = (ends[-1] // tm).astype(jnp.int32).reshape(1)
    tile_start = jnp.minimum(jnp.arange(n_tiles, dtype=jnp.int32), n_used[0] - 1) * tm
    tile_expert = jnp.sum(tile_start[:, None] >= ends[None, :], axis=1).astype(jnp.int32)

    pad_lo = jnp.concatenate([starts + counts, ends[-1:]]).astype(jnp.int32)
    pad_hi = jnp.concatenate([ends, jnp.full((1,), n_slots, ends.dtype)]).astype(jnp.int32)
    xg = dispatch(x, moe_norm, rows, pad_lo, pad_hi, n_slots)
    act = matmul(xg, [w_gate_e.astype(BF16), w_up_e.astype(BF16)], epilogue="swiglu",
                 tm=tm, tn=512, tile_expert=tile_expert, n_used=n_used, name="moe_gate_up")
    yg = matmul(act, [w_down_e.astype(BF16)], out_dtype=F32, tm=tm, tn=1024, tk=6144,
                tile_expert=tile_expert, n_used=n_used, name="moe_down")
    return combine(x, info, yg, rows)


def kernel(x, attn_norm, w_qkv, q_norm, k_norm, sinks, w_o, ffn_norm, w_gate, w_up, w_down,
           ssm_norm, lam_re, lam_im, log_step, b_re, b_im, c_re, c_im, d_skip, w_glu,
           moe_norm, w_router, w_gate_e, w_up_e, w_down_e):
    b, s, d = x.shape
    depth = attn_norm.shape[0] + ssm_norm.shape[0]
    outs = []
    for bi in range(b):
        xb = x[bi]
        for i in range(depth):
            j = i // 2
            if i % 2 == 0:
                xb = _attention_layer(xb, attn_norm[j], w_qkv[j], q_norm[j], k_norm[j],
                                      sinks[j], w_o[j])
                later = (w_gate_e[j], w_up_e[j], w_down_e[j]) if i + 1 < depth else ()
                xb, experts_bf16 = _ffn_layer(xb, ffn_norm[j], w_gate[j], w_up[j], w_down[j],
                                              later)
            else:
                xb = _s5_layer(xb, ssm_norm[j], lam_re[j], lam_im[j], log_step[j], b_re[j],
                               b_im[j], c_re[j], c_im[j], d_skip[j], w_glu[j])
                xb = _moe_layer(xb, moe_norm[j], w_router[j], *experts_bf16)
        outs.append(xb)
    return jnp.stack(outs, axis=0)
```

```python
import functools

import jax
import jax.numpy as jnp
from jax import lax
from jax.experimental import pallas as pl
from jax.experimental.pallas import tpu as pltpu

F32 = jnp.float32
BF16 = jnp.bfloat16
HIGHEST = lax.Precision.HIGHEST

RMS_EPS = 1e-5
NEG_INF = -1e30

HEAD_DIM = 128
Q_PER_KV = 4
ATTN_BLOCK = 128
SSM_GROUP = 16
SSM_STATE = 64
SSM_CHUNK = 16
TOP_K = 2

LANES = 128
VMEM_LIMIT_MB = 56


def _cparams(semantics, vmem_mb=VMEM_LIMIT_MB, row_dma_loop=False):
    return pltpu.CompilerParams(dimension_semantics=semantics,
                                vmem_limit_bytes=vmem_mb << 20,
                                disable_bounds_checks=row_dma_loop)


def _tile(n, pref, quantum=128):
    if n <= pref:
        return n
    t = (pref // quantum) * quantum
    while t > quantum and n % t:
        t -= quantum
    assert n % t == 0, (n, pref)
    return t


def _rms(x, gain):
    ms = jnp.mean(x * x, axis=-1, keepdims=True)
    return x * lax.rsqrt(ms + RMS_EPS) * gain


def _sigmoid(x):
    return 1.0 / (1.0 + jnp.exp(-x))


def _rmsnorm_body(x_ref, g_ref, o_ref):
    o_ref[...] = _rms(x_ref[...], g_ref[...]).astype(o_ref.dtype)


def rmsnorm(x, gain, rows=256):
    s, d = x.shape
    tr = _tile(s, rows, 8)
    return pl.pallas_call(
        _rmsnorm_body,
        grid=(s // tr,),
        in_specs=[pl.BlockSpec((tr, d), lambda i: (i, 0)),
                  pl.BlockSpec((1, d), lambda i: (0, 0))],
        out_specs=pl.BlockSpec((tr, d), lambda i: (i, 0)),
        out_shape=jax.ShapeDtypeStruct((s, d), BF16),
        compiler_params=_cparams(("parallel",)),
        name="rmsnorm",
    )(x, gain.reshape(1, d).astype(F32))


def _side_specs(side, grid):
    steps = 1
    for g in grid:
        steps *= g
    specs = []
    for arr in side:
        rows, cols = arr.shape
        nblk = max(c for c in range(1, steps + 1) if rows % c == 0 and (rows // c) % 16 == 0)

        def s_map(*idx, nblk=nblk):
            lin = idx[0]
            for d, g in zip(idx[1:len(grid)], grid[1:]):
                lin = lin * g + d
            return (jnp.minimum(lin, nblk - 1), 0)
        specs.append(pl.BlockSpec((rows // nblk, cols), s_map))
    return specs


def _side_shapes(side):
    return [jax.ShapeDtypeStruct(arr.shape, BF16) for arr in side]


def _side_convert(side_in, side_out):
    for src, dst in zip(side_in, side_out):
        dst[...] = src[...].astype(dst.dtype)


def _flat2d(w):
    return w.reshape(-1, w.shape[-1])


def _mm_body(*refs, n_rhs, nk, epilogue, has_res, grouped, n_side):
    pos = 0
    if grouped:
        nused_ref = refs[1]
        pos = 2
    a_ref = refs[pos]
    w_refs = refs[pos + 1:pos + 1 + n_rhs]
    pos += 1 + n_rhs
    res_ref = refs[pos] if has_res else None
    pos += int(has_res)
    side_in = refs[pos:pos + n_side]
    pos += n_side
    o_ref = refs[pos]
    side_out = refs[pos + 1:pos + 1 + n_side]
    acc_refs = refs[pos + 1 + n_side:]
    k = pl.program_id(2)

    _side_convert(side_in, side_out)

    def finish(parts):
        if epilogue == "swiglu":
            y = parts[0] * _sigmoid(parts[0]) * parts[1]
        elif epilogue == "glu":
            y = parts[0] * _sigmoid(parts[1])
        else:
            y = parts[0]
        if has_res:
            y = res_ref[...] + y
        o_ref[...] = y.astype(o_ref.dtype)

    def compute():
        a = a_ref[...].astype(BF16)
        parts = [jnp.dot(a, w[...], preferred_element_type=F32) for w in w_refs]
        if nk == 1:
            finish(parts)
            return
        if not acc_refs:
            @pl.when(k == 0)
            def _():
                o_ref[...] = res_ref[...] if has_res else jnp.zeros(o_ref.shape, F32)

            o_ref[...] += parts[0]
            return

        @pl.when(k == 0)
        def _():
            for acc, p in zip(acc_refs, parts):
                acc[...] = p

        @pl.when(k > 0)
        def _():
            for acc, p in zip(acc_refs, parts):
                acc[...] += p

        @pl.when(k == nk - 1)
        def _():
            finish([acc[...] for acc in acc_refs])

    if grouped:
        valid = pl.program_id(0) < nused_ref[0]
        pl.when(valid)(compute)

        @pl.when(jnp.logical_not(valid) & (k == nk - 1))
        def _():
            o_ref[...] = jnp.zeros(o_ref.shape, o_ref.dtype)
    else:
        compute()


def matmul(a, ws, *, res=None, epilogue=None, out_dtype=BF16, tm=1024, tn=1024, tk=4096,
           w_col_offsets=None, n_out=None, tile_expert=None, n_used=None, side=(),
           name="matmul"):
    m, kdim = a.shape
    grouped = tile_expert is not None
    n_total = ws[0].shape[-1]
    n = n_total if n_out is None else n_out
    offs = [0] * len(ws) if w_col_offsets is None else w_col_offsets
    tm, tn, tk = _tile(m, tm), _tile(n, tn), _tile(kdim, tk)
    nk = kdim // tk
    grid = (m // tm, n // tn, nk)
    assert all(o % tn == 0 for o in offs)

    if grouped:
        def a_map(i, j, k, te, nu):
            v = i < nu[0]
            return (jnp.where(v, i, nu[0] - 1), jnp.where(v, k, 0))

        def w_map(off):
            def f(i, j, k, te, nu):
                v = i < nu[0]
                return (te[i], jnp.where(v, k, 0), jnp.where(v, j, 0) + off // tn)
            return f

        def o_map(i, j, k, te, nu):
            return (i, j)
        w_specs = [pl.BlockSpec((None, tk, tn), w_map(o)) for o in offs]
    else:
        def a_map(i, j, k):
            return (i, k)

        def w_map(off):
            return lambda i, j, k: (k, j + off // tn)

        def o_map(i, j, k):
            return (i, j)
        w_specs = [pl.BlockSpec((tk, tn), w_map(o)) for o in offs]

    in_specs = [pl.BlockSpec((tm, tk), a_map)] + w_specs
    args = [a] + list(ws)
    if res is not None:
        in_specs.append(pl.BlockSpec((tm, tn), o_map))
        args.append(res)
    in_place = epilogue is None and out_dtype == F32 and not grouped
    scratch = [pltpu.VMEM((tm, tn), F32) for _ in ws] if nk > 1 and not in_place else []
    body = functools.partial(_mm_body, n_rhs=len(ws), nk=nk, epilogue=epilogue,
                             has_res=res is not None, grouped=grouped, n_side=len(side))
    out_shape = jax.ShapeDtypeStruct((m, n), out_dtype)
    sem = ("parallel", "parallel", "arbitrary")
    if side:
        assert not grouped
        outs = pl.pallas_call(body, grid=grid, in_specs=in_specs + _side_specs(side, grid),
                              out_specs=[pl.BlockSpec((tm, tn), o_map)] + _side_specs(side, grid),
                              out_shape=[out_shape] + _side_shapes(side),
                              scratch_shapes=scratch,
                              compiler_params=_cparams(("arbitrary",) * 3),
                              name=name)(*args, *side)
        return outs[0], list(outs[1:])
    if grouped:
        grid_spec = pltpu.PrefetchScalarGridSpec(
            num_scalar_prefetch=2, grid=grid, in_specs=in_specs,
            out_specs=pl.BlockSpec((tm, tn), o_map), scratch_shapes=scratch)
        return pl.pallas_call(body, grid_spec=grid_spec, out_shape=out_shape,
                              compiler_params=_cparams(sem), name=name)(
                                  tile_expert, n_used, *args)
    return pl.pallas_call(body, grid=grid, in_specs=in_specs,
                          out_specs=pl.BlockSpec((tm, tn), o_map), out_shape=out_shape,
                          scratch_shapes=scratch, compiler_params=_cparams(sem),
                          name=name)(*args)


def _attn_body(sink_ref, q_ref, kc_ref, kp_ref, vc_ref, vp_ref, qg_ref, kg_ref, *rest, tq):
    n_side = (len(rest) - 1) // 2
    o_ref = rest[n_side]
    _side_convert(rest[:n_side], rest[n_side + 1:])
    i = pl.program_id(0)
    g = pl.program_id(1)
    blk = ATTN_BLOCK
    scale = HEAD_DIM ** -0.5
    qg = qg_ref[...]
    kg = kg_ref[...]
    k_all = jnp.concatenate([_rms(kp_ref[...].astype(F32), kg).astype(BF16),
                             _rms(kc_ref[...].astype(F32), kg).astype(BF16)], axis=0)
    v_all = jnp.concatenate([vp_ref[...], vc_ref[...]], axis=0)

    rows = Q_PER_KV * blk
    r = lax.broadcasted_iota(jnp.int32, (rows, 2 * blk), 0)
    c = lax.broadcasted_iota(jnp.int32, (rows, 2 * blk), 1)
    diff = (r % blk) + blk - c
    band = (diff >= 0) & (diff < blk)
    head = lax.broadcasted_iota(jnp.int32, (rows, 1), 0) // blk
    sink = jnp.zeros((rows, 1), F32)
    for a in range(Q_PER_KV):
        sink = jnp.where(head == a, sink_ref[g * Q_PER_KV + a], sink)

    for j in range(tq // blk):
        k2 = k_all[j * blk:(j + 2) * blk]
        v2 = v_all[j * blk:(j + 2) * blk]
        q = jnp.concatenate(
            [q_ref[j * blk:(j + 1) * blk, a * HEAD_DIM:(a + 1) * HEAD_DIM]
             for a in range(Q_PER_KV)], axis=0)
        qn = _rms(q.astype(F32), qg).astype(BF16)
        s = lax.dot_general(qn, k2, (((1,), (1,)), ((), ())),
                            preferred_element_type=F32) * scale
        if j == 0:
            valid = band & (c >= jnp.where(i > 0, 0, blk))
        else:
            valid = band
        s = jnp.where(valid, s, NEG_INF)
        m = jnp.maximum(jnp.max(s, axis=-1, keepdims=True), sink)
        e = jnp.exp(s - m)
        denom = jnp.sum(e, axis=-1, keepdims=True) + jnp.exp(sink - m)
        p = (e / denom).astype(BF16)
        o = jnp.dot(p, v2, preferred_element_type=F32)
        for a in range(Q_PER_KV):
            o_ref[j * blk:(j + 1) * blk, a * HEAD_DIM:(a + 1) * HEAD_DIM] = (
                o[a * blk:(a + 1) * blk].astype(o_ref.dtype))


def attention(qkv, q_gain, k_gain, sinks, n_q_heads, tq=512, side=()):
    s = qkv.shape[0]
    n_kv = n_q_heads // Q_PER_KV
    tq = _tile(s, tq, ATTN_BLOCK)
    sub = tq // ATTN_BLOCK
    qw = Q_PER_KV * HEAD_DIM
    k0 = n_q_heads
    v0 = n_q_heads + n_kv
    grid = (s // tq, n_kv)
    grid_spec = pltpu.PrefetchScalarGridSpec(
        num_scalar_prefetch=1,
        grid=grid,
        in_specs=[
            pl.BlockSpec((tq, qw), lambda i, g, sk: (i, g)),
            pl.BlockSpec((tq, HEAD_DIM), lambda i, g, sk: (i, k0 + g)),
            pl.BlockSpec((ATTN_BLOCK, HEAD_DIM),
                         lambda i, g, sk: (jnp.maximum(i * sub - 1, 0), k0 + g)),
            pl.BlockSpec((tq, HEAD_DIM), lambda i, g, sk: (i, v0 + g)),
            pl.BlockSpec((ATTN_BLOCK, HEAD_DIM),
                         lambda i, g, sk: (jnp.maximum(i * sub - 1, 0), v0 + g)),
            pl.BlockSpec((1, HEAD_DIM), lambda i, g, sk: (0, 0)),
            pl.BlockSpec((1, HEAD_DIM), lambda i, g, sk: (0, 0)),
        ] + _side_specs(side, grid),
        out_specs=[pl.BlockSpec((tq, qw), lambda i, g, sk: (i, g))] + _side_specs(side, grid),
    )
    outs = pl.pallas_call(
        functools.partial(_attn_body, tq=tq),
        grid_spec=grid_spec,
        out_shape=[jax.ShapeDtypeStruct((s, n_q_heads * HEAD_DIM), BF16)] + _side_shapes(side),
        compiler_params=_cparams(("arbitrary", "arbitrary")),
        name="swa_attention",
    )(sinks.astype(F32), qkv, qkv, qkv, qkv, qkv,
      q_gain.reshape(1, HEAD_DIM).astype(F32), k_gain.reshape(1, HEAD_DIM).astype(F32), *side)
    return outs[0], list(outs[1:])


SLAB_GROUPS = LANES // SSM_GROUP
CHUNK_PAIRS = SSM_CHUNK // 2
STATE_LANES = 2 * SSM_STATE


def _s5_group_tables(prm):
    p2 = STATE_LANES
    lc = SSM_CHUNK * SSM_GROUP
    bt = prm[0:16]
    cc = prm[16:32]
    lam_re = prm[32:33]
    lam_im = prm[33:34]
    step = jnp.exp(prm[34:35])
    lane = lax.broadcasted_iota(jnp.int32, (1, p2), 1)
    sgn = jnp.where(lane < SSM_STATE, -1.0, 1.0).astype(F32)

    def swap(x):
        return pltpu.roll(x, SSM_STATE, 1)

    def powers(expo):
        mag = jnp.exp(expo * (lam_re * step))
        ang = expo * (lam_im * step)
        return mag * jnp.cos(ang), mag * jnp.sin(ang)

    row = lax.broadcasted_iota(jnp.int32, (48, 1), 0)
    expo = jnp.where(row < 32, row, 16 * (1 << jnp.maximum(row - 32, 0))).astype(F32)
    pr, pi = powers(expo)

    lbr, lbi = pr[1:2], pi[1:2]
    inv = 1.0 / (lam_re * lam_re + lam_im * lam_im)
    cf_re = ((lbr - 1.0) * lam_re + lbi * lam_im) * inv
    cf_im = (lbi * lam_re - (lbr - 1.0) * lam_im) * inv
    bb1 = cf_re * bt + sgn * cf_im * swap(bt)
    bb2 = swap(bb1)
    ca = jnp.where(lane < SSM_STATE, cc, -cc)
    cb = -swap(cc)

    def rep_rows(x):
        return jnp.broadcast_to(x[:, None, :], (16, SSM_GROUP, p2)).reshape(lc, p2)

    def tile_rows(x):
        return jnp.broadcast_to(x[None, :, :], (SSM_CHUNK, 16, p2)).reshape(lc, p2)

    rrow = lax.broadcasted_iota(jnp.int32, (16, 1), 0)
    qr, qi = powers((SSM_CHUNK - 1 - rrow).astype(F32))
    s_mat = tile_rows(bb1) * rep_rows(qr) + tile_rows(bb2) * rep_rows(sgn * qi)
    c_t = tile_rows(ca) * rep_rows(pr[1:17]) + tile_rows(cb) * rep_rows(pi[1:17])
    f_t = tile_rows(ca) * rep_rows(pr[0:16]) + tile_rows(cb) * rep_rows(pi[0:16])
    kern = lax.dot_general(bb1, f_t, (((1,), (1,)), ((), ())),
                           precision=HIGHEST, preferred_element_type=F32)
    return s_mat, c_t, kern, pr, pi, sgn


def _s5_body(x_ref, prm_ref, dsk_ref, y_ref, xs, xp, wt, ws, wc):
    nc = x_ref.shape[0] // SSM_CHUNK
    gw = SSM_GROUP
    sl = STATE_LANES

    @pl.when(pl.program_id(0) == 0)
    def _():
        ws[...] = jnp.zeros(ws.shape, ws.dtype)
        wc[...] = jnp.zeros(wc.shape, wc.dtype)

    lane_w = lax.broadcasted_iota(jnp.int32, (gw, 2 * LANES), 1)
    scan_tables = []
    for g in range(SLAB_GROUPS):
        s_mat, c_t, kern, pr, pi, sgn = _s5_group_tables(prm_ref[g])
        scan_tables.append((pr, pi, sgn))
        for s in range(SSM_CHUNK):
            r0 = (s % 2) * LANES + g * gw
            ws[s // 2, r0:r0 + gw, g * sl:(g + 1) * sl] = s_mat[s * gw:(s + 1) * gw].astype(BF16)
            wc[s // 2, r0:r0 + gw, g * sl:(g + 1) * sl] = c_t[s * gw:(s + 1) * gw].astype(BF16)
        for d in range(CHUNK_PAIRS):
            for h in range(2):
                blk = jnp.zeros((gw, 2 * LANES), F32)
                for h2 in range(2):
                    tau = 2 * d + h2 - h
                    if 0 <= tau < SSM_CHUNK:
                        dst = h2 * LANES + g * gw
                        shift = (dst - tau * gw) % (2 * LANES)
                        piece = pltpu.roll(kern, shift, 1) if shift else kern
                        blk = jnp.where((lane_w >= dst) & (lane_w < dst + gw), piece, blk)
                r0 = h * LANES + g * gw
                wt[d, r0:r0 + gw, :] = blk.astype(BF16)

    xs[...] = x_ref[...].astype(F32)
    for sp in range(CHUNK_PAIRS):
        xp[sp] = jnp.concatenate(
            [xs[pl.ds(2 * sp, nc, stride=SSM_CHUNK), :],
             xs[pl.ds(2 * sp + 1, nc, stride=SSM_CHUNK), :]], axis=1).astype(BF16)

    v = jnp.dot(xp[0], ws[0], preferred_element_type=F32)
    for sp in range(1, CHUNK_PAIRS):
        v = v + jnp.dot(xp[sp], ws[sp], preferred_element_type=F32)

    rown = lax.broadcasted_iota(jnp.int32, (nc, 1), 0)
    h_in = []
    for g in range(SLAB_GROUPS):
        pr, pi, sgn = scan_tables[g]
        x = v[:, g * sl:(g + 1) * sl]
        d, j = 1, 0
        while d < nc:
            ar = pr[32 + j:33 + j]
            ai = sgn * pi[32 + j:33 + j]
            sh = jnp.where(rown >= d, pltpu.roll(x, d, 0), 0.0)
            x = x + ar * sh + ai * pltpu.roll(sh, SSM_STATE, 1)
            d, j = d * 2, j + 1
        h_in.append(jnp.where(rown >= 1, pltpu.roll(x, 1, 0), 0.0).astype(BF16))
    h_in = jnp.concatenate(h_in, axis=1)

    dsk = dsk_ref[...]
    for tp in range(CHUNK_PAIRS):
        acc = lax.dot_general(h_in, wc[tp], (((1,), (1,)), ((), ())),
                              preferred_element_type=F32)
        for sp in range(tp + 1):
            acc = acc + jnp.dot(xp[sp], wt[tp - sp], preferred_element_type=F32)
        for h2 in range(2):
            rows = pl.ds(2 * tp + h2, nc, stride=SSM_CHUNK)
            z = acc[:, h2 * LANES:(h2 + 1) * LANES] + dsk * xs[rows, :]
            xs[rows, :] = 0.5 * z * (1.0 + jnp.tanh(
                0.7978845608028654 * (z + 0.044715 * (z * z * z))))
    y_ref[...] = xs[...].astype(y_ref.dtype)


def s5_mix(h, prm, d_skip):
    s, d = h.shape
    nc = s // SSM_CHUNK
    return pl.pallas_call(
        _s5_body,
        grid=(d // LANES,),
        in_specs=[pl.BlockSpec((s, LANES), lambda j: (0, j)),
                  pl.BlockSpec((SLAB_GROUPS, 40, STATE_LANES), lambda j: (j, 0, 0)),
                  pl.BlockSpec((1, LANES), lambda j: (0, j))],
        out_specs=pl.BlockSpec((s, LANES), lambda j: (0, j)),
        out_shape=jax.ShapeDtypeStruct((s, d), BF16),
        scratch_shapes=[
            pltpu.VMEM((s, LANES), F32),
            pltpu.VMEM((CHUNK_PAIRS, nc, 2 * LANES), BF16),
            pltpu.VMEM((CHUNK_PAIRS, 2 * LANES, 2 * LANES), BF16),
            pltpu.VMEM((CHUNK_PAIRS, 2 * LANES, SLAB_GROUPS * STATE_LANES), BF16),
            pltpu.VMEM((CHUNK_PAIRS, 2 * LANES, SLAB_GROUPS * STATE_LANES), BF16),
        ],
        compiler_params=_cparams(("arbitrary",)),
        name="s5_mix",
    )(h, prm, d_skip.reshape(1, d).astype(F32))


def _router_body(x_ref, g_ref, wr_ref, info_ref, cnt_ref, carry_ref, *, n_exp):
    i = pl.program_id(0)
    tr = x_ref.shape[0]

    @pl.when(i == 0)
    def _():
        carry_ref[...] = jnp.zeros(carry_ref.shape, F32)

    h = _rms(x_ref[...], g_ref[...])
    logits = jnp.dot(h, wr_ref[...], precision=HIGHEST, preferred_element_type=F32)
    lane = lax.broadcasted_iota(jnp.int32, (tr, LANES), 1).astype(F32)
    neg = jnp.float32(-jnp.inf)
    l1 = jnp.where(lane < n_exp, logits, neg)
    m1 = jnp.max(l1, axis=-1, keepdims=True)
    i1 = jnp.min(jnp.where(l1 == m1, lane, float(LANES)), axis=-1, keepdims=True)
    l2 = jnp.where(lane == i1, neg, l1)
    m2 = jnp.max(l2, axis=-1, keepdims=True)
    i2 = jnp.min(jnp.where(l2 == m2, lane, float(LANES)), axis=-1, keepdims=True)
    e = jnp.exp(m2 - m1)
    w1 = 1.0 / (1.0 + e)
    w2 = e / (1.0 + e)

    hit1 = lane == i1
    hit2 = lane == i2
    onehot = jnp.where(hit1 | hit2, 1.0, 0.0)
    rr = lax.broadcasted_iota(jnp.int32, (tr, tr), 0)
    cc = lax.broadcasted_iota(jnp.int32, (tr, tr), 1)
    lower = jnp.where(rr > cc, 1.0, 0.0).astype(BF16)
    before = jnp.dot(lower, onehot.astype(BF16), preferred_element_type=F32) + carry_ref[0:1]
    r1 = jnp.sum(jnp.where(hit1, before, 0.0), axis=-1, keepdims=True)
    r2 = jnp.sum(jnp.where(hit2, before, 0.0), axis=-1, keepdims=True)
    total = carry_ref[0:1] + jnp.sum(onehot, axis=0, keepdims=True)
    carry_ref[...] = jnp.broadcast_to(total, carry_ref.shape)
    cnt_ref[...] = jnp.broadcast_to(total, cnt_ref.shape)

    info = jnp.where(lane == 0, i1, 0.0)
    for idx, val in ((1, i2), (2, r1), (3, r2), (4, w1), (5, w2)):
        info = jnp.where(lane == idx, val, info)
    info_ref[...] = info


def router(x, gain, w_router, rows=256):
    s, d = x.shape
    n_exp = w_router.shape[1]
    tr = _tile(s, rows, 8)
    wr = jnp.zeros((d, LANES), F32).at[:, :n_exp].set(w_router.astype(F32))
    return pl.pallas_call(
        functools.partial(_router_body, n_exp=n_exp),
        grid=(s // tr,),
        in_specs=[pl.BlockSpec((tr, d), lambda i: (i, 0)),
                  pl.BlockSpec((1, d), lambda i: (0, 0)),
                  pl.BlockSpec((d, LANES), lambda i: (0, 0))],
        out_specs=[pl.BlockSpec((tr, LANES), lambda i: (i, 0)),
                   pl.BlockSpec((8, LANES), lambda i: (0, 0))],
        out_shape=[jax.ShapeDtypeStruct((s, LANES), F32),
                   jax.ShapeDtypeStruct((8, LANES), F32)],
        scratch_shapes=[pltpu.VMEM((8, LANES), F32)],
        compiler_params=_cparams(("arbitrary",)),
        name="moe_router",
    )(x, gain.reshape(1, d).astype(F32), wr)


def _dispatch_body(rows_ref, lo_ref, hi_ref, x_ref, g_ref, xg_ref, hbuf, zrow, sem, *, tt,
                   n_ranges):
    base = pl.program_id(0) * tt

    @pl.when(pl.program_id(0) == 0)
    def _():
        zrow[...] = jnp.zeros(zrow.shape, zrow.dtype)

        def zcopy(r):
            return pltpu.make_async_copy(zrow.at[pl.ds(0, 1), :], xg_ref.at[pl.ds(r, 1), :], sem)

        for e in range(n_ranges):
            lax.fori_loop(lo_ref[e], hi_ref[e], lambda r, c: (zcopy(r).start(), c)[1], 0)
        for e in range(n_ranges):
            lax.fori_loop(lo_ref[e], hi_ref[e], lambda r, c: (zcopy(r).wait(), c)[1], 0)

    hbuf[...] = _rms(x_ref[...], g_ref[...])

    def copy(t, k):
        return pltpu.make_async_copy(
            hbuf.at[pl.ds(t, 1), :],
            xg_ref.at[pl.ds(rows_ref[(base + t) * TOP_K + k], 1), :], sem)

    def start(t, c):
        for k in range(TOP_K):
            copy(t, k).start()
        return c

    def wait(t, c):
        for k in range(TOP_K):
            copy(t, k).wait()
        return c

    lax.fori_loop(0, tt, start, 0)
    lax.fori_loop(0, tt, wait, 0)


def dispatch(x, gain, rows, pad_lo, pad_hi, n_slots, tokens=256):
    s, d = x.shape
    tt = _tile(s, tokens, 8)
    grid_spec = pltpu.PrefetchScalarGridSpec(
        num_scalar_prefetch=3,
        grid=(s // tt,),
        in_specs=[pl.BlockSpec((tt, d), lambda i, *_: (i, 0)),
                  pl.BlockSpec((1, d), lambda i, *_: (0, 0))],
        out_specs=pl.BlockSpec(memory_space=pl.ANY),
        scratch_shapes=[pltpu.VMEM((tt, d), F32), pltpu.VMEM((8, d), F32),
                        pltpu.SemaphoreType.DMA(())],
    )
    return pl.pallas_call(
        functools.partial(_dispatch_body, tt=tt, n_ranges=pad_lo.shape[0]),
        grid_spec=grid_spec,
        out_shape=jax.ShapeDtypeStruct((n_slots, d), F32),
        compiler_params=_cparams(("arbitrary",), row_dma_loop=True),
        name="moe_dispatch",
    )(rows, pad_lo, pad_hi, x, gain.reshape(1, d).astype(F32))


def _combine_body(rows_ref, x_ref, w_ref, y_ref, o_ref, buf, sem, *, tt):
    base = pl.program_id(0) * tt

    def copy(t, k):
        return pltpu.make_async_copy(
            y_ref.at[pl.ds(rows_ref[(base + t) * TOP_K + k], 1), :],
            buf.at[k, pl.ds(t, 1), :], sem)

    def start(t, c):
        for k in range(TOP_K):
            copy(t, k).start()
        return c

    def wait(t, c):
        for k in range(TOP_K):
            copy(t, k).wait()
        return c

    lax.fori_loop(0, tt, start, 0)
    lax.fori_loop(0, tt, wait, 0)
    w = w_ref[...]
    o_ref[...] = x_ref[...] + w[:, 4:5] * buf[0] + w[:, 5:6] * buf[1]


def combine(x, info, y, rows, tokens=256):
    s, d = x.shape
    tt = _tile(s, tokens, 8)
    grid_spec = pltpu.PrefetchScalarGridSpec(
        num_scalar_prefetch=1,
        grid=(s // tt,),
        in_specs=[pl.BlockSpec((tt, d), lambda i, r: (i, 0)),
                  pl.BlockSpec((tt, LANES), lambda i, r: (i, 0)),
                  pl.BlockSpec(memory_space=pl.ANY)],
        out_specs=pl.BlockSpec((tt, d), lambda i, r: (i, 0)),
        scratch_shapes=[pltpu.VMEM((TOP_K, tt, d), F32), pltpu.SemaphoreType.DMA(())],
    )
    return pl.pallas_call(
        functools.partial(_combine_body, tt=tt),
        grid_spec=grid_spec,
        out_shape=jax.ShapeDtypeStruct((s, d), F32),
        compiler_params=_cparams(("arbitrary",), row_dma_loop=True),
        name="moe_combine",
    )(rows, x, info, y)


def _matmul_converting(a, ws, later_weights, **kw):
    if not later_weights:
        return matmul(a, ws, **kw), []
    out, conv = matmul(a, ws, side=[_flat2d(w) for w in later_weights], **kw)
    return out, [c.reshape(w.shape) for c, w in zip(conv, later_weights)]


def _attention_layer(x, attn_norm, w_qkv, q_norm, k_norm, sinks, w_o, later_weights=()):
    n_q_heads = w_o.shape[0] // HEAD_DIM
    h = rmsnorm(x, attn_norm)
    qkv, (w_o16,) = _matmul_converting(h, [w_qkv.astype(BF16)], [w_o], name="qkv_proj")
    o, conv = attention(qkv, q_norm, k_norm, sinks, n_q_heads,
                        side=[_flat2d(w) for w in later_weights])
    conv = [c.reshape(w.shape) for c, w in zip(conv, later_weights)]
    return matmul(o, [w_o16], res=x, out_dtype=F32, name="attn_out_proj"), conv


def _ffn_layer(x, ffn_norm, w_gate, w_up, w_down, under_gate_up=(), under_down=()):
    h = rmsnorm(x, ffn_norm)
    act, conv_a = _matmul_converting(h, [w_gate.astype(BF16), w_up.astype(BF16)],
                                     under_gate_up, epilogue="swiglu", tn=256,
                                     name="ffn_gate_up")
    out, conv_b = _matmul_converting(act, [w_down.astype(BF16)], under_down, res=x,
                                     out_dtype=F32, tk=3584, name="ffn_down")
    return out, conv_a, conv_b


def _s5_layer(x, ssm_norm, lam_re, lam_im, log_step, b_re, b_im, c_re, c_im, d_skip, w_glu):
    s, d = x.shape
    g = d // SSM_GROUP
    h = rmsnorm(x, ssm_norm)

    def two(a, b):
        return jnp.concatenate([a, b], axis=-1).astype(F32)
    prm = jnp.concatenate([
        two(b_re.transpose(0, 2, 1), b_im.transpose(0, 2, 1)),
        two(c_re, c_im),
        two(lam_re, lam_re)[:, None, :],
        two(lam_im, lam_im)[:, None, :],
        jnp.broadcast_to(log_step.astype(F32)[:, None, None], (g, 6, 2 * SSM_STATE)),
    ], axis=1)
    y = s5_mix(h, prm, d_skip)
    wg = w_glu.astype(BF16)
    return matmul(y, [wg, wg], epilogue="glu", res=x, out_dtype=F32, tn=512,
                  w_col_offsets=[0, d], n_out=d, name="s5_glu")


def _moe_layer(x, moe_norm, w_router, w_gate_e, w_up_e, w_down_e, tm=512):
    s, d = x.shape
    n_exp = w_router.shape[1]
    info, cnt = router(x, moe_norm, w_router)

    counts = cnt[0, :n_exp].astype(jnp.int32)
    padded = ((counts + tm - 1) // tm) * tm
    ends = jnp.cumsum(padded)
    starts = ends - padded
    n_slots = s * TOP_K + n_exp * tm
    n_tiles = n_slots // tm
    experts = info[:, 0:TOP_K].astype(jnp.int32)
    ranks = info[:, TOP_K:2 * TOP_K].astype(jnp.int32)
    rows = (starts[experts] + ranks).reshape(s * TOP_K)
    n_used = (ends[-1] // tm).astype(jnp.int32).reshape(1)
    tile_start = jnp.minimum(jnp.arange(n_tiles, dtype=jnp.int32), n_used[0] - 1) * tm
    tile_expert = jnp.sum(tile_start[:, None] >= ends[None, :], axis=1).astype(jnp.int32)

    pad_lo = jnp.concatenate([starts + counts, ends[-1:]]).astype(jnp.int32)
    pad_hi = jnp.concatenate([ends, jnp.full((1,), n_slots, ends.dtype)]).astype(jnp.int32)
    xg = dispatch(x, moe_norm, rows, pad_lo, pad_hi, n_slots)
    act = matmul(xg, [w_gate_e.astype(BF16), w_up_e.astype(BF16)], epilogue="swiglu",
                 tm=tm, tn=512, tile_expert=tile_expert, n_used=n_used, name="moe_gate_up")
    yg = matmul(act, [w_down_e.astype(BF16)], out_dtype=F32, tm=tm, tn=1024, tk=6144,
                tile_expert=tile_expert, n_used=n_used, name="moe_down")
    return combine(x, info, yg, rows)


def kernel(x, attn_norm, w_qkv, q_norm, k_norm, sinks, w_o, ffn_norm, w_gate, w_up, w_down,
           ssm_norm, lam_re, lam_im, log_step, b_re, b_im, c_re, c_im, d_skip, w_glu,
           moe_norm, w_router, w_gate_e, w_up_e, w_down_e):
    b, s, d = x.shape
    depth = attn_norm.shape[0] + ssm_norm.shape[0]
    outs = []
    for bi in range(b):
        xb = x[bi]
        for i in range(depth):
            j = i // 2
            if i % 2 == 0:
                xb, ffn16 = _attention_layer(xb, attn_norm[j], w_qkv[j], q_norm[j], k_norm[j],
                                             sinks[j], w_o[j], [w_gate[j], w_up[j], w_down[j]])
                has_next = i + 1 < depth
                xb, experts16, glu16 = _ffn_layer(
                    xb, ffn_norm[j], *ffn16,
                    under_gate_up=[w_gate_e[j], w_up_e[j], w_down_e[j]] if has_next else [],
                    under_down=[w_glu[j]] if has_next else [])
            else:
                xb = _s5_layer(xb, ssm_norm[j], lam_re[j], lam_im[j], log_step[j], b_re[j],
                               b_im[j], c_re[j], c_im[j], d_skip[j], *glu16)
                xb = _moe_layer(xb, moe_norm[j], w_router[j], *experts16)
        outs.append(xb)
    return jnp.stack(outs, axis=0)
```

```python
import functools

import jax
import jax.numpy as jnp
from jax import lax
from jax.experimental import pallas as pl
from jax.experimental.pallas import tpu as pltpu

F32 = jnp.float32
BF16 = jnp.bfloat16
HIGHEST = lax.Precision.HIGHEST

RMS_EPS = 1e-5
NEG_INF = -1e30

HEAD_DIM = 128
Q_PER_KV = 4
ATTN_BLOCK = 128
SSM_GROUP = 16
SSM_STATE = 64
SSM_CHUNK = 16
TOP_K = 2

LANES = 128
VMEM_LIMIT_MB = 56


def _cparams(semantics, vmem_mb=VMEM_LIMIT_MB):
    return pltpu.CompilerParams(dimension_semantics=semantics,
                                vmem_limit_bytes=vmem_mb << 20)


def _tile(n, pref, quantum=128):
    if n <= pref:
        return n
    t = (pref // quantum) * quantum
    while t > quantum and n % t:
        t -= quantum
    assert n % t == 0, (n, pref)
    return t


def _rms(x, gain):
    ms = jnp.mean(x * x, axis=-1, keepdims=True)
    return x * lax.rsqrt(ms + RMS_EPS) * gain


def _sigmoid(x):
    return 1.0 / (1.0 + jnp.exp(-x))


def _rmsnorm_body(x_ref, g_ref, o_ref):
    o_ref[...] = _rms(x_ref[...], g_ref[...]).astype(o_ref.dtype)


def rmsnorm(x, gain, rows=256):
    s, d = x.shape
    tr = _tile(s, rows, 8)
    return pl.pallas_call(
        _rmsnorm_body,
        grid=(s // tr,),
        in_specs=[pl.BlockSpec((tr, d), lambda i: (i, 0)),
                  pl.BlockSpec((1, d), lambda i: (0, 0))],
        out_specs=pl.BlockSpec((tr, d), lambda i: (i, 0)),
        out_shape=jax.ShapeDtypeStruct((s, d), BF16),
        compiler_params=_cparams(("parallel",)),
        name="rmsnorm",
    )(x, gain.reshape(1, d).astype(F32))


def _side_specs(side, grid):
    steps = 1
    for g in grid:
        steps *= g
    specs = []
    for arr in side:
        rows, cols = arr.shape
        nblk = max(c for c in range(1, steps + 1) if rows % c == 0 and (rows // c) % 16 == 0)

        def s_map(*idx, nblk=nblk):
            lin = idx[0]
            for d, g in zip(idx[1:len(grid)], grid[1:]):
                lin = lin * g + d
            return (jnp.minimum(lin, nblk - 1), 0)
        specs.append(pl.BlockSpec((rows // nblk, cols), s_map))
    return specs


def _side_shapes(side):
    return [jax.ShapeDtypeStruct(arr.shape, BF16) for arr in side]


def _side_convert(side_in, side_out):
    for src, dst in zip(side_in, side_out):
        dst[...] = src[...].astype(dst.dtype)


def _flat2d(w):
    return w.reshape(-1, w.shape[-1])


def _mm_body(*refs, n_rhs, nk, epilogue, has_res, grouped, n_side, cache_lhs):
    a16_ref = None
    if cache_lhs:
        a16_ref, refs = refs[-1], refs[:-1]
    pos = 0
    if grouped:
        nused_ref = refs[1]
        pos = 2
    a_ref = refs[pos]
    w_refs = refs[pos + 1:pos + 1 + n_rhs]
    pos += 1 + n_rhs
    res_ref = refs[pos] if has_res else None
    pos += int(has_res)
    side_in = refs[pos:pos + n_side]
    pos += n_side
    o_ref = refs[pos]
    side_out = refs[pos + 1:pos + 1 + n_side]
    acc_refs = refs[pos + 1 + n_side:]
    k = pl.program_id(2)

    _side_convert(side_in, side_out)

    def finish(parts):
        if epilogue == "swiglu":
            y = parts[0] * _sigmoid(parts[0]) * parts[1]
        elif epilogue == "glu":
            y = parts[0] * _sigmoid(parts[1])
        else:
            y = parts[0]
        if has_res:
            y = res_ref[...] + y
        o_ref[...] = y.astype(o_ref.dtype)

    def compute():
        if cache_lhs:
            @pl.when(pl.program_id(1) == 0)
            def _():
                a16_ref[...] = a_ref[...].astype(BF16)
            a = a16_ref[...]
        else:
            a = a_ref[...].astype(BF16)
        parts = [jnp.dot(a, w[...], preferred_element_type=F32) for w in w_refs]
        if nk == 1:
            finish(parts)
            return
        if not acc_refs:
            @pl.when(k == 0)
            def _():
                o_ref[...] = res_ref[...] if has_res else jnp.zeros(o_ref.shape, F32)

            o_ref[...] += parts[0]
            return

        @pl.when(k == 0)
        def _():
            for acc, p in zip(acc_refs, parts):
                acc[...] = p

        @pl.when(k > 0)
        def _():
            for acc, p in zip(acc_refs, parts):
                acc[...] += p

        @pl.when(k == nk - 1)
        def _():
            finish([acc[...] for acc in acc_refs])

    if grouped:
        valid = pl.program_id(0) < nused_ref[0]
        pl.when(valid)(compute)

        @pl.when(jnp.logical_not(valid) & (k == nk - 1))
        def _():
            o_ref[...] = jnp.zeros(o_ref.shape, o_ref.dtype)
    else:
        compute()


def matmul(a, ws, *, res=None, epilogue=None, out_dtype=BF16, tm=1024, tn=1024, tk=4096,
           w_col_offsets=None, n_out=None, tile_expert=None, n_used=None, side=(),
           name="matmul"):
    m, kdim = a.shape
    grouped = tile_expert is not None
    n_total = ws[0].shape[-1]
    n = n_total if n_out is None else n_out
    offs = [0] * len(ws) if w_col_offsets is None else w_col_offsets
    tm, tn, tk = _tile(m, tm), _tile(n, tn), _tile(kdim, tk)
    nk = kdim // tk
    grid = (m // tm, n // tn, nk)
    assert all(o % tn == 0 for o in offs)

    if grouped:
        def a_map(i, j, k, te, nu):
            v = i < nu[0]
            return (jnp.where(v, i, nu[0] - 1), jnp.where(v, k, 0))

        def w_map(off):
            def f(i, j, k, te, nu):
                v = i < nu[0]
                return (te[i], jnp.where(v, k, 0), jnp.where(v, j, 0) + off // tn)
            return f

        def o_map(i, j, k, te, nu):
            return (i, j)
        w_specs = [pl.BlockSpec((None, tk, tn), w_map(o)) for o in offs]
    else:
        def a_map(i, j, k):
            return (i, k)

        def w_map(off):
            return lambda i, j, k: (k, j + off // tn)

        def o_map(i, j, k):
            return (i, j)
        w_specs = [pl.BlockSpec((tk, tn), w_map(o)) for o in offs]

    in_specs = [pl.BlockSpec((tm, tk), a_map)] + w_specs
    args = [a] + list(ws)
    if res is not None:
        in_specs.append(pl.BlockSpec((tm, tn), o_map))
        args.append(res)
    in_place = epilogue is None and out_dtype == F32 and not grouped
    scratch = [pltpu.VMEM((tm, tn), F32) for _ in ws] if nk > 1 and not in_place else []
    cache_lhs = a.dtype == F32 and nk == 1 and n // tn > 1
    if cache_lhs:
        scratch.append(pltpu.VMEM((tm, tk), BF16))
    body = functools.partial(_mm_body, n_rhs=len(ws), nk=nk, epilogue=epilogue,
                             has_res=res is not None, grouped=grouped, n_side=len(side),
                             cache_lhs=cache_lhs)
    out_shape = jax.ShapeDtypeStruct((m, n), out_dtype)
    sem = ("parallel", "arbitrary" if cache_lhs else "parallel", "arbitrary")
    if side:
        assert not grouped
        outs = pl.pallas_call(body, grid=grid, in_specs=in_specs + _side_specs(side, grid),
                              out_specs=[pl.BlockSpec((tm, tn), o_map)] + _side_specs(side, grid),
                              out_shape=[out_shape] + _side_shapes(side),
                              scratch_shapes=scratch,
                              compiler_params=_cparams(("arbitrary",) * 3),
                              name=name)(*args, *side)
        return outs[0], list(outs[1:])
    if grouped:
        grid_spec = pltpu.PrefetchScalarGridSpec(
            num_scalar_prefetch=2, grid=grid, in_specs=in_specs,
            out_specs=pl.BlockSpec((tm, tn), o_map), scratch_shapes=scratch)
        return pl.pallas_call(body, grid_spec=grid_spec, out_shape=out_shape,
                              compiler_params=_cparams(sem), name=name)(
                                  tile_expert, n_used, *args)
    return pl.pallas_call(body, grid=grid, in_specs=in_specs,
                          out_specs=pl.BlockSpec((tm, tn), o_map), out_shape=out_shape,
                          scratch_shapes=scratch, compiler_params=_cparams(sem),
                          name=name)(*args)


def _attn_body(sink_ref, q_ref, kc_ref, kp_ref, vc_ref, vp_ref, qg_ref, kg_ref, *rest, tq):
    n_side = (len(rest) - 1) // 2
    o_ref = rest[n_side]
    _side_convert(rest[:n_side], rest[n_side + 1:])
    i = pl.program_id(0)
    g = pl.program_id(1)
    blk = ATTN_BLOCK
    scale = HEAD_DIM ** -0.5
    qg = qg_ref[...]
    kg = kg_ref[...]
    k_all = jnp.concatenate([_rms(kp_ref[...].astype(F32), kg).astype(BF16),
                             _rms(kc_ref[...].astype(F32), kg).astype(BF16)], axis=0)
    v_all = jnp.concatenate([vp_ref[...], vc_ref[...]], axis=0)

    rows = Q_PER_KV * blk
    r = lax.broadcasted_iota(jnp.int32, (rows, 2 * blk), 0)
    c = lax.broadcasted_iota(jnp.int32, (rows, 2 * blk), 1)
    diff = (r % blk) + blk - c
    band = (diff >= 0) & (diff < blk)
    head = lax.broadcasted_iota(jnp.int32, (rows, 1), 0) // blk
    sink = jnp.zeros((rows, 1), F32)
    for a in range(Q_PER_KV):
        sink = jnp.where(head == a, sink_ref[g * Q_PER_KV + a], sink)

    for j in range(tq // blk):
        k2 = k_all[j * blk:(j + 2) * blk]
        v2 = v_all[j * blk:(j + 2) * blk]
        q = jnp.concatenate(
            [q_ref[j * blk:(j + 1) * blk, a * HEAD_DIM:(a + 1) * HEAD_DIM]
             for a in range(Q_PER_KV)], axis=0)
        qn = _rms(q.astype(F32), qg).astype(BF16)
        s = lax.dot_general(qn, k2, (((1,), (1,)), ((), ())),
                            preferred_element_type=F32) * scale
        if j == 0:
            valid = band & (c >= jnp.where(i > 0, 0, blk))
        else:
            valid = band
        s = jnp.where(valid, s, NEG_INF)
        m = jnp.maximum(jnp.max(s, axis=-1, keepdims=True), sink)
        e = jnp.exp(s - m)
        denom = jnp.sum(e, axis=-1, keepdims=True) + jnp.exp(sink - m)
        p = (e / denom).astype(BF16)
        o = jnp.dot(p, v2, preferred_element_type=F32)
        for a in range(Q_PER_KV):
            o_ref[j * blk:(j + 1) * blk, a * HEAD_DIM:(a + 1) * HEAD_DIM] = (
                o[a * blk:(a + 1) * blk].astype(o_ref.dtype))


def attention(qkv, q_gain, k_gain, sinks, n_q_heads, tq=512, side=()):
    s = qkv.shape[0]
    n_kv = n_q_heads // Q_PER_KV
    tq = _tile(s, tq, ATTN_BLOCK)
    sub = tq // ATTN_BLOCK
    qw = Q_PER_KV * HEAD_DIM
    k0 = n_q_heads
    v0 = n_q_heads + n_kv
    grid = (s // tq, n_kv)
    grid_spec = pltpu.PrefetchScalarGridSpec(
        num_scalar_prefetch=1,
        grid=grid,
        in_specs=[
            pl.BlockSpec((tq, qw), lambda i, g, sk: (i, g)),
            pl.BlockSpec((tq, HEAD_DIM), lambda i, g, sk: (i, k0 + g)),
            pl.BlockSpec((ATTN_BLOCK, HEAD_DIM),
                         lambda i, g, sk: (jnp.maximum(i * sub - 1, 0), k0 + g)),
            pl.BlockSpec((tq, HEAD_DIM), lambda i, g, sk: (i, v0 + g)),
            pl.BlockSpec((ATTN_BLOCK, HEAD_DIM),
                         lambda i, g, sk: (jnp.maximum(i * sub - 1, 0), v0 + g)),
            pl.BlockSpec((1, HEAD_DIM), lambda i, g, sk: (0, 0)),
            pl.BlockSpec((1, HEAD_DIM), lambda i, g, sk: (0, 0)),
        ] + _side_specs(side, grid),
        out_specs=[pl.BlockSpec((tq, qw), lambda i, g, sk: (i, g))] + _side_specs(side, grid),
    )
    outs = pl.pallas_call(
        functools.partial(_attn_body, tq=tq),
        grid_spec=grid_spec,
        out_shape=[jax.ShapeDtypeStruct((s, n_q_heads * HEAD_DIM), BF16)] + _side_shapes(side),
        compiler_params=_cparams(("arbitrary", "arbitrary")),
        name="swa_attention",
    )(sinks.astype(F32), qkv, qkv, qkv, qkv, qkv,
      q_gain.reshape(1, HEAD_DIM).astype(F32), k_gain.reshape(1, HEAD_DIM).astype(F32), *side)
    return outs[0], list(outs[1:])


SLAB_GROUPS = LANES // SSM_GROUP
CHUNK_PAIRS = SSM_CHUNK // 2
STATE_LANES = 2 * SSM_STATE
SCAN_BLOCK = 8


def _s5_group_tables(prm):
    p2 = STATE_LANES
    lc = SSM_CHUNK * SSM_GROUP
    bt = prm[0:16]
    cc = prm[16:32]
    lam_re = prm[32:33]
    lam_im = prm[33:34]
    step = jnp.exp(prm[34:35])
    lane = lax.broadcasted_iota(jnp.int32, (1, p2), 1)
    sgn = jnp.where(lane < SSM_STATE, -1.0, 1.0).astype(F32)

    def swap(x):
        return pltpu.roll(x, SSM_STATE, 1)

    def powers(expo):
        mag = jnp.exp(expo * (lam_re * step))
        ang = expo * (lam_im * step)
        return mag * jnp.cos(ang), mag * jnp.sin(ang)

    row = lax.broadcasted_iota(jnp.int32, (48, 1), 0)
    expo = jnp.where(row < 17, row,
                     jnp.where(row < 32, SSM_CHUNK * (row - 16),
                               SSM_CHUNK * (1 << jnp.maximum(row - 32, 0)))).astype(F32)
    pr, pi = powers(expo)

    lbr, lbi = pr[1:2], pi[1:2]
    inv = 1.0 / (lam_re * lam_re + lam_im * lam_im)
    cf_re = ((lbr - 1.0) * lam_re + lbi * lam_im) * inv
    cf_im = (lbi * lam_re - (lbr - 1.0) * lam_im) * inv
    bb1 = cf_re * bt + sgn * cf_im * swap(bt)
    bb2 = swap(bb1)
    ca = jnp.where(lane < SSM_STATE, cc, -cc)
    cb = -swap(cc)

    def rep_rows(x):
        return jnp.broadcast_to(x[:, None, :], (16, SSM_GROUP, p2)).reshape(lc, p2)

    def tile_rows(x):
        return jnp.broadcast_to(x[None, :, :], (SSM_CHUNK, 16, p2)).reshape(lc, p2)

    rrow = lax.broadcasted_iota(jnp.int32, (16, 1), 0)
    qr, qi = powers((SSM_CHUNK - 1 - rrow).astype(F32))
    s_mat = tile_rows(bb1) * rep_rows(qr) + tile_rows(bb2) * rep_rows(sgn * qi)
    c_t = tile_rows(ca) * rep_rows(pr[1:17]) + tile_rows(cb) * rep_rows(pi[1:17])
    f_t = tile_rows(ca) * rep_rows(pr[0:16]) + tile_rows(cb) * rep_rows(pi[0:16])
    kern = lax.dot_general(bb1, f_t, (((1,), (1,)), ((), ())),
                           precision=HIGHEST, preferred_element_type=F32)
    return s_mat, c_t, kern, pr, pi, sgn


def _s5_body(x_ref, prm_ref, dsk_ref, y_ref, xs, vb, xp, wt, ws, wc):
    nc = x_ref.shape[0] // SSM_CHUNK
    gw = SSM_GROUP
    sl = STATE_LANES

    @pl.when(pl.program_id(0) == 0)
    def _():
        ws[...] = jnp.zeros(ws.shape, ws.dtype)
        wc[...] = jnp.zeros(wc.shape, wc.dtype)

    lane_w = lax.broadcasted_iota(jnp.int32, (gw, 2 * LANES), 1)
    scan_tables = []
    for g in range(SLAB_GROUPS):
        s_mat, c_t, kern, pr, pi, sgn = _s5_group_tables(prm_ref[g])
        scan_tables.append((pr, pi, sgn))
        for s in range(SSM_CHUNK):
            r0 = (s % 2) * LANES + g * gw
            ws[s // 2, r0:r0 + gw, g * sl:(g + 1) * sl] = s_mat[s * gw:(s + 1) * gw].astype(BF16)
            wc[s // 2, r0:r0 + gw, g * sl:(g + 1) * sl] = c_t[s * gw:(s + 1) * gw].astype(BF16)
        for d in range(CHUNK_PAIRS):
            for h in range(2):
                blk = jnp.zeros((gw, 2 * LANES), F32)
                for h2 in range(2):
                    tau = 2 * d + h2 - h
                    if 0 <= tau < SSM_CHUNK:
                        dst = h2 * LANES + g * gw
                        shift = (dst - tau * gw) % (2 * LANES)
                        piece = pltpu.roll(kern, shift, 1) if shift else kern
                        blk = jnp.where((lane_w >= dst) & (lane_w < dst + gw), piece, blk)
                r0 = h * LANES + g * gw
                wt[d, r0:r0 + gw, :] = blk.astype(BF16)

    xs[...] = x_ref[...].astype(F32)
    for sp in range(CHUNK_PAIRS):
        xp[sp] = jnp.concatenate(
            [xs[pl.ds(2 * sp, nc, stride=SSM_CHUNK), :],
             xs[pl.ds(2 * sp + 1, nc, stride=SSM_CHUNK), :]], axis=1).astype(BF16)

    v = jnp.dot(xp[0], ws[0], preferred_element_type=F32)
    for sp in range(1, CHUNK_PAIRS):
        v = v + jnp.dot(xp[sp], ws[sp], preferred_element_type=F32)

    nb = nc // SCAN_BLOCK
    rown = lax.broadcasted_iota(jnp.int32, (nc, 1), 0)
    phase = rown % SCAN_BLOCK
    rowb = lax.broadcasted_iota(jnp.int32, (nb, 1), 0)

    def cmul_add(x, sh, ar, ai):
        return x + ar * sh + ai * pltpu.roll(sh, SSM_STATE, 1)

    h_in = []
    for g in range(SLAB_GROUPS):
        pr, pi, sgn = scan_tables[g]
        pis = sgn * pi
        x = v[:, g * sl:(g + 1) * sl]
        d, j = 1, 0
        while d < SCAN_BLOCK:
            sh = jnp.where(phase >= d, pltpu.roll(x, d, 0), 0.0)
            x = cmul_add(x, sh, pr[32 + j:33 + j], pis[32 + j:33 + j])
            d, j = d * 2, j + 1
        vb[...] = x
        t = vb[pl.ds(SCAN_BLOCK - 1, nb, stride=SCAN_BLOCK), :]
        d = 1
        while d < nb:
            sh = jnp.where(rowb >= d, pltpu.roll(t, d, 0), 0.0)
            t = cmul_add(t, sh, pr[32 + j:33 + j], pis[32 + j:33 + j])
            d, j = d * 2, j + 1
        before = jnp.where(rowb >= 1, pltpu.roll(t, 1, 0), 0.0)
        before = jnp.broadcast_to(before[:, None, :], (nb, SCAN_BLOCK, sl)).reshape(nc, sl)

        def per_phase(tab):
            return jnp.broadcast_to(tab[None, 17:17 + SCAN_BLOCK, :],
                                    (nb, SCAN_BLOCK, sl)).reshape(nc, sl)
        x = cmul_add(x, before, per_phase(pr), per_phase(pis))
        h_in.append(jnp.where(rown >= 1, pltpu.roll(x, 1, 0), 0.0).astype(BF16))
    h_in = jnp.concatenate(h_in, axis=1)

    dsk = dsk_ref[...]
    for tp in range(CHUNK_PAIRS):
        acc = lax.dot_general(h_in, wc[tp], (((1,), (1,)), ((), ())),
                              preferred_element_type=F32)
        for sp in range(tp + 1):
            acc = acc + jnp.dot(xp[sp], wt[tp - sp], preferred_element_type=F32)
        for h2 in range(2):
            rows = pl.ds(2 * tp + h2, nc, stride=SSM_CHUNK)
            z = acc[:, h2 * LANES:(h2 + 1) * LANES] + dsk * xs[rows, :]
            xs[rows, :] = 0.5 * z * (1.0 + jnp.tanh(
                0.7978845608028654 * (z + 0.044715 * (z * z * z))))
    y_ref[...] = xs[...].astype(y_ref.dtype)


def s5_mix(h, prm, d_skip):
    s, d = h.shape
    nc = s // SSM_CHUNK
    return pl.pallas_call(
        _s5_body,
        grid=(d // LANES,),
        in_specs=[pl.BlockSpec((s, LANES), lambda j: (0, j)),
                  pl.BlockSpec((SLAB_GROUPS, 40, STATE_LANES), lambda j: (j, 0, 0)),
                  pl.BlockSpec((1, LANES), lambda j: (0, j))],
        out_specs=pl.BlockSpec((s, LANES), lambda j: (0, j)),
        out_shape=jax.ShapeDtypeStruct((s, d), BF16),
        scratch_shapes=[
            pltpu.VMEM((s, LANES), F32),
            pltpu.VMEM((nc, STATE_LANES), F32),
            pltpu.VMEM((CHUNK_PAIRS, nc, 2 * LANES), BF16),
            pltpu.VMEM((CHUNK_PAIRS, 2 * LANES, 2 * LANES), BF16),
            pltpu.VMEM((CHUNK_PAIRS, 2 * LANES, SLAB_GROUPS * STATE_LANES), BF16),
            pltpu.VMEM((CHUNK_PAIRS, 2 * LANES, SLAB_GROUPS * STATE_LANES), BF16),
        ],
        compiler_params=_cparams(("arbitrary",)),
        name="s5_mix",
    )(h, prm, d_skip.reshape(1, d).astype(F32))


def _router_body(x_ref, g_ref, wr_ref, info_ref, cnt_ref, carry_ref, *, n_exp):
    i = pl.program_id(0)
    tr = x_ref.shape[0]

    @pl.when(i == 0)
    def _():
        carry_ref[...] = jnp.zeros(carry_ref.shape, F32)

    h = _rms(x_ref[...], g_ref[...])
    logits = jnp.dot(h, wr_ref[...], precision=HIGHEST, preferred_element_type=F32)
    lane = lax.broadcasted_iota(jnp.int32, (tr, LANES), 1).astype(F32)
    neg = jnp.float32(-jnp.inf)
    l1 = jnp.where(lane < n_exp, logits, neg)
    m1 = jnp.max(l1, axis=-1, keepdims=True)
    i1 = jnp.min(jnp.where(l1 == m1, lane, float(LANES)), axis=-1, keepdims=True)
    l2 = jnp.where(lane == i1, neg, l1)
    m2 = jnp.max(l2, axis=-1, keepdims=True)
    i2 = jnp.min(jnp.where(l2 == m2, lane, float(LANES)), axis=-1, keepdims=True)
    e = jnp.exp(m2 - m1)
    w1 = 1.0 / (1.0 + e)
    w2 = e / (1.0 + e)

    hit1 = lane == i1
    hit2 = lane == i2
    onehot = jnp.where(hit1 | hit2, 1.0, 0.0)
    rr = lax.broadcasted_iota(jnp.int32, (tr, tr), 0)
    cc = lax.broadcasted_iota(jnp.int32, (tr, tr), 1)
    lower = jnp.where(rr > cc, 1.0, 0.0).astype(BF16)
    before = jnp.dot(lower, onehot.astype(BF16), preferred_element_type=F32) + carry_ref[0:1]
    r1 = jnp.sum(jnp.where(hit1, before, 0.0), axis=-1, keepdims=True)
    r2 = jnp.sum(jnp.where(hit2, before, 0.0), axis=-1, keepdims=True)
    total = carry_ref[0:1] + jnp.sum(onehot, axis=0, keepdims=True)
    carry_ref[...] = jnp.broadcast_to(total, carry_ref.shape)
    cnt_ref[...] = jnp.broadcast_to(total, cnt_ref.shape)

    info = jnp.where(lane == 0, i1, 0.0)
    for idx, val in ((1, i2), (2, r1), (3, r2), (4, w1), (5, w2)):
        info = jnp.where(lane == idx, val, info)
    info_ref[...] = info


def router(x, gain, w_router, rows=256):
    s, d = x.shape
    n_exp = w_router.shape[1]
    tr = _tile(s, rows, 8)
    wr = jnp.zeros((d, LANES), F32).at[:, :n_exp].set(w_router.astype(F32))
    return pl.pallas_call(
        functools.partial(_router_body, n_exp=n_exp),
        grid=(s // tr,),
        in_specs=[pl.BlockSpec((tr, d), lambda i: (i, 0)),
                  pl.BlockSpec((1, d), lambda i: (0, 0)),
                  pl.BlockSpec((d, LANES), lambda i: (0, 0))],
        out_specs=[pl.BlockSpec((tr, LANES), lambda i: (i, 0)),
                   pl.BlockSpec((8, LANES), lambda i: (0, 0))],
        out_shape=[jax.ShapeDtypeStruct((s, LANES), F32),
                   jax.ShapeDtypeStruct((8, LANES), F32)],
        scratch_shapes=[pltpu.VMEM((8, LANES), F32)],
        compiler_params=_cparams(("arbitrary",)),
        name="moe_router",
    )(x, gain.reshape(1, d).astype(F32), wr)


def _dispatch_body(rows_ref, lo_ref, hi_ref, x_ref, g_ref, xg_ref, hbuf, zrow, sem, *, tt,
                   n_ranges):
    base = pl.program_id(0) * tt

    @pl.when(pl.program_id(0) == 0)
    def _():
        zrow[...] = jnp.zeros(zrow.shape, zrow.dtype)

        def zcopy(r):
            return pltpu.make_async_copy(zrow.at[pl.ds(0, 1), :], xg_ref.at[pl.ds(r, 1), :], sem)

        for e in range(n_ranges):
            lax.fori_loop(lo_ref[e], hi_ref[e], lambda r, c: (zcopy(r).start(), c)[1], 0)
        for e in range(n_ranges):
            lax.fori_loop(lo_ref[e], hi_ref[e], lambda r, c: (zcopy(r).wait(), c)[1], 0)

    hbuf[...] = _rms(x_ref[...], g_ref[...])

    def copy(t, k):
        return pltpu.make_async_copy(
            hbuf.at[pl.ds(t, 1), :],
            xg_ref.at[pl.ds(rows_ref[(base + t) * TOP_K + k], 1), :], sem)

    def start(t, c):
        for k in range(TOP_K):
            copy(t, k).start()
        return c

    def wait(t, c):
        for k in range(TOP_K):
            copy(t, k).wait()
        return c

    lax.fori_loop(0, tt, start, 0)
    lax.fori_loop(0, tt, wait, 0)


def dispatch(x, gain, rows, pad_lo, pad_hi, n_slots, tokens=256):
    s, d = x.shape
    tt = _tile(s, tokens, 8)
    grid_spec = pltpu.PrefetchScalarGridSpec(
        num_scalar_prefetch=3,
        grid=(s // tt,),
        in_specs=[pl.BlockSpec((tt, d), lambda i, *_: (i, 0)),
                  pl.BlockSpec((1, d), lambda i, *_: (0, 0))],
        out_specs=pl.BlockSpec(memory_space=pl.ANY),
        scratch_shapes=[pltpu.VMEM((tt, d), F32), pltpu.VMEM((8, d), F32),
                        pltpu.SemaphoreType.DMA(())],
    )
    return pl.pallas_call(
        functools.partial(_dispatch_body, tt=tt, n_ranges=pad_lo.shape[0]),
        grid_spec=grid_spec,
        out_shape=jax.ShapeDtypeStruct((n_slots, d), F32),
        compiler_params=_cparams(("arbitrary",)),
        name="moe_dispatch",
    )(rows, pad_lo, pad_hi, x, gain.reshape(1, d).astype(F32))


def _combine_body(rows_ref, x_ref, w_ref, y_ref, o_ref, buf, sem, *, tt):
    base = pl.program_id(0) * tt

    def copy(t, k):
        return pltpu.make_async_copy(
            y_ref.at[pl.ds(rows_ref[(base + t) * TOP_K + k], 1), :],
            buf.at[k, pl.ds(t, 1), :], sem)

    def start(t, c):
        for k in range(TOP_K):
            copy(t, k).start()
        return c

    def wait(t, c):
        for k in range(TOP_K):
            copy(t, k).wait()
        return c

    lax.fori_loop(0, tt, start, 0)
    lax.fori_loop(0, tt, wait, 0)
    w = w_ref[...]
    o_ref[...] = x_ref[...] + w[:, 4:5] * buf[0] + w[:, 5:6] * buf[1]


def combine(x, info, y, rows, tokens=256):
    s, d = x.shape
    tt = _tile(s, tokens, 8)
    grid_spec = pltpu.PrefetchScalarGridSpec(
        num_scalar_prefetch=1,
        grid=(s // tt,),
        in_specs=[pl.BlockSpec((tt, d), lambda i, r: (i, 0)),
                  pl.BlockSpec((tt, LANES), lambda i, r: (i, 0)),
                  pl.BlockSpec(memory_space=pl.ANY)],
        out_specs=pl.BlockSpec((tt, d), lambda i, r: (i, 0)),
        scratch_shapes=[pltpu.VMEM((TOP_K, tt, d), F32), pltpu.SemaphoreType.DMA(())],
    )
    return pl.pallas_call(
        functools.partial(_combine_body, tt=tt),
        grid_spec=grid_spec,
        out_shape=jax.ShapeDtypeStruct((s, d), F32),
        compiler_params=_cparams(("arbitrary",)),
        name="moe_combine",
    )(rows, x, info, y)


def _matmul_converting(a, ws, later_weights, **kw):
    if not later_weights:
        return matmul(a, ws, **kw), []
    out, conv = matmul(a, ws, side=[_flat2d(w) for w in later_weights], **kw)
    return out, [c.reshape(w.shape) for c, w in zip(conv, later_weights)]


def _attention_layer(x, attn_norm, w_qkv, q_norm, k_norm, sinks, w_o, later_weights=()):
    n_q_heads = w_o.shape[0] // HEAD_DIM
    h = rmsnorm(x, attn_norm)
    qkv, (w_o16,) = _matmul_converting(h, [w_qkv.astype(BF16)], [w_o], name="qkv_proj")
    o, conv = attention(qkv, q_norm, k_norm, sinks, n_q_heads,
                        side=[_flat2d(w) for w in later_weights])
    conv = [c.reshape(w.shape) for c, w in zip(conv, later_weights)]
    return matmul(o, [w_o16], res=x, out_dtype=F32, name="attn_out_proj"), conv


def _ffn_layer(x, ffn_norm, w_gate, w_up, w_down, under_gate_up=(), under_down=()):
    h = rmsnorm(x, ffn_norm)
    act, conv_a = _matmul_converting(h, [w_gate.astype(BF16), w_up.astype(BF16)],
                                     under_gate_up, epilogue="swiglu", tn=256,
                                     name="ffn_gate_up")
    out, conv_b = _matmul_converting(act, [w_down.astype(BF16)], under_down, res=x,
                                     out_dtype=F32, tk=3584, name="ffn_down")
    return out, conv_a, conv_b


def _s5_layer(x, ssm_norm, lam_re, lam_im, log_step, b_re, b_im, c_re, c_im, d_skip, w_glu):
    s, d = x.shape
    g = d // SSM_GROUP
    h = rmsnorm(x, ssm_norm)

    def two(a, b):
        return jnp.concatenate([a, b], axis=-1).astype(F32)
    prm = jnp.concatenate([
        two(b_re.transpose(0, 2, 1), b_im.transpose(0, 2, 1)),
        two(c_re, c_im),
        two(lam_re, lam_re)[:, None, :],
        two(lam_im, lam_im)[:, None, :],
        jnp.broadcast_to(log_step.astype(F32)[:, None, None], (g, 6, 2 * SSM_STATE)),
    ], axis=1)
    y = s5_mix(h, prm, d_skip)
    wg = w_glu.astype(BF16)
    return matmul(y, [wg, wg], epilogue="glu", res=x, out_dtype=F32, tn=512,
                  w_col_offsets=[0, d], n_out=d, name="s5_glu")


def _moe_layer(x, moe_norm, w_router, w_gate_e, w_up_e, w_down_e, tm=512):
    s, d = x.shape
    n_exp = w_router.shape[1]
    info, cnt = router(x, moe_norm, w_router)

    counts = cnt[0, :n_exp].astype(jnp.int32)
    padded = ((counts + tm - 1) // tm) * tm
    ends = jnp.cumsum(padded)
    starts = ends - padded
    n_slots = s * TOP_K + n_exp * tm
    n_tiles = n_slots // tm
    experts = info[:, 0:TOP_K].astype(jnp.int32)
    ranks = info[:, TOP_K:2 * TOP_K].astype(jnp.int32)
    rows = (starts[experts] + ranks).reshape(s * TOP_K)
    n_used = (ends[-1] // tm).astype(jnp.int32).reshape(1)
    tile_start = jnp.minimum(jnp.arange(n_tiles, dtype=jnp.int32), n_used[0] - 1) * tm
    tile_expert = jnp.sum(tile_start[:, None] >= ends[None, :], axis=1).astype(jnp.int32)

    pad_lo = jnp.concatenate([starts + counts, ends[-1:]]).astype(jnp.int32)
    pad_hi = jnp.concatenate([ends, jnp.full((1,), n_slots, ends.dtype)]).astype(jnp.int32)
    xg = dispatch(x, moe_norm, rows, pad_lo, pad_hi, n_slots)
    act = matmul(xg, [w_gate_e.astype(BF16), w_up_e.astype(BF16)], epilogue="swiglu",
                 tm=tm, tn=512, tile_expert=tile_expert, n_used=n_used, name="moe_gate_up")
    yg = matmul(act, [w_down_e.astype(BF16)], out_dtype=F32, tm=tm, tn=1024, tk=6144,
                tile_expert=tile_expert, n_used=n_used, name="moe_down")
    return combine(x, info, yg, rows)


def kernel(x, attn_norm, w_qkv, q_norm, k_norm, sinks, w_o, ffn_norm, w_gate, w_up, w_down,
           ssm_norm, lam_re, lam_im, log_step, b_re, b_im, c_re, c_im, d_skip, w_glu,
           moe_norm, w_router, w_gate_e, w_up_e, w_down_e):
    b, s, d = x.shape
    depth = attn_norm.shape[0] + ssm_norm.shape[0]
    outs = []
    for bi in range(b):
        xb = x[bi]
        for i in range(depth):
            j = i // 2
            if i % 2 == 0:
                xb, ffn16 = _attention_layer(xb, attn_norm[j], w_qkv[j], q_norm[j], k_norm[j],
                                             sinks[j], w_o[j], [w_gate[j], w_up[j], w_down[j]])
                has_next = i + 1 < depth
                xb, experts16, glu16 = _ffn_layer(
                    xb, ffn_norm[j], *ffn16,
                    under_gate_up=[w_gate_e[j], w_up_e[j], w_down_e[j]] if has_next else [],
                    under_down=[w_glu[j]] if has_next else [])
            else:
                xb = _s5_layer(xb, ssm_norm[j], lam_re[j], lam_im[j], log_step[j], b_re[j],
                               b_im[j], c_re[j], c_im[j], d_skip[j], *glu16)
                xb = _moe_layer(xb, moe_norm[j], w_router[j], *experts16)
        outs.append(xb)
    return jnp.stack(outs, axis=0)
```
